```python
import jax, jax.numpy as jnp
from jax import lax
import numpy as np

D_MODEL = 2048
BATCH = 4
SEQ = 4096
DEPTH = 4

GRID_W = 64
CTX_LEN = 256
N_MIXERS = 3
N_MOD = 9
D_FF = 5632
RMS_EPS = 1e-6
ROPE_BASE = 10000.0
NEG_INF = -1e30

NA_HEADS = 16
NA_HEAD_DIM = 128
NA_KH = 8
NA_KW = 16

MLA_HEADS = 16
MLA_Q_LORA = 512
MLA_KV_LORA = 512
MLA_NOPE = 128
MLA_ROPE = 64
MLA_V = 128
MLA_BLOCK = 128

SWA_HEADS = 16
SWA_KV_HEADS = 4
SWA_HEAD_DIM = 128
SWA_WINDOW = 128
SWA_BLOCK = 128

kernel_name = 'hybrid_na_mla_swa_macaron_dit'


def rmsnorm(x, g):
    xf = x.astype(jnp.float32)
    y = xf * lax.rsqrt(jnp.mean(xf * xf, axis=-1, keepdims=True) + RMS_EPS)
    return (y * g.astype(jnp.float32)).astype(x.dtype)


def modulate(x, g, shift, scale):
    return rmsnorm(x, g) * (1 + scale) + shift


def adaln_params(cond, w, b):
    m = jax.nn.silu(cond) @ w + b
    return jnp.split(m, N_MOD, axis=-1)


def swiglu(h, w_in, w_out):
    gate, up = jnp.split(h @ w_in, 2, axis=-1)
    return (jax.nn.silu(gate) * up) @ w_out


def axial_rope_tables(n_tokens, rot_dim):
    t = jnp.arange(n_tokens)
    row = (t // GRID_W).astype(jnp.float32)
    col = (t % GRID_W).astype(jnp.float32)
    n_freq = rot_dim // 4
    inv_freq = ROPE_BASE ** (-jnp.arange(n_freq, dtype=jnp.float32) / n_freq)
    ang = jnp.concatenate([row[:, None] * inv_freq, col[:, None] * inv_freq], axis=-1)
    return jnp.cos(ang), jnp.sin(ang)


def apply_rope(x, cos, sin):
    half = x.shape[-1] // 2
    xf = x.astype(jnp.float32)
    x1, x2 = xf[..., :half], xf[..., half:]
    return jnp.concatenate([x1 * cos - x2 * sin, x1 * sin + x2 * cos], axis=-1).astype(x.dtype)


def softmax_with_sink(s, sink):
    m = jnp.maximum(jnp.max(s, axis=-1, keepdims=True), sink)
    e = jnp.exp(s - m)
    return e / (jnp.sum(e, axis=-1, keepdims=True) + jnp.exp(sink - m))


def context_attention(q, k, v, scale):
    s = jnp.einsum('bqhd,bkhd->bhqk', q, k).astype(jnp.float32) * scale
    p = jax.nn.softmax(s, axis=-1).astype(v.dtype)
    return jnp.einsum('bhqk,bkhd->bqhd', p, v)


def neighbourhood_attention(hx, hz, w_qkv, rpb, w_o, need_ctx):
    B, S, _ = hx.shape
    T = hz.shape[1]
    rows = S // GRID_W
    kh = min(NA_KH, rows)
    H, dh = NA_HEADS, NA_HEAD_DIM
    scale = dh ** -0.5
    qkv = (hx @ w_qkv).reshape(B, S, 3, H, dh)
    q, k, v = qkv[:, :, 0], qkv[:, :, 1], qkv[:, :, 2]
    qkvz = (hz @ w_qkv).reshape(B, T, 3, H, dh)
    qz, kz, vz = qkvz[:, :, 0], qkvz[:, :, 1], qkvz[:, :, 2]

    col = jnp.arange(GRID_W)
    col_start = jnp.clip(col - NA_KW // 2, 0, GRID_W - NA_KW)
    in_win = (col[None, :] >= col_start[:, None]) & (col[None, :] < col_start[:, None] + NA_KW)
    mask = jnp.broadcast_to(in_win[:, None, :], (GRID_W, kh, GRID_W)).reshape(GRID_W, kh * GRID_W)
    col_idx = jnp.clip(col[None, :] - col[:, None] + NA_KW - 1, 0, 2 * NA_KW - 2)
    q_rows = jnp.moveaxis(q.reshape(B, rows, GRID_W, H, dh), 1, 0)
    n_lat = kh * GRID_W

    def row_block(args):
        q_r, r = args
        rs = jnp.clip(r - kh // 2, 0, rows - kh)
        k_band = lax.dynamic_slice_in_dim(k, rs * GRID_W, n_lat, axis=1)
        v_band = lax.dynamic_slice_in_dim(v, rs * GRID_W, n_lat, axis=1)
        row_idx = rs + jnp.arange(kh) - r + NA_KH - 1
        bias = rpb[:, row_idx[:, None, None], col_idx[None, :, :]]
        bias = jnp.transpose(bias, (0, 2, 1, 3)).reshape(H, GRID_W, n_lat)
        s_lat = jnp.einsum('bqhd,bkhd->bhqk', q_r, k_band).astype(jnp.float32) * scale + bias.astype(jnp.float32)
        s_lat = jnp.where(mask, s_lat, NEG_INF)
        s_ctx = jnp.einsum('bqhd,bkhd->bhqk', q_r, kz).astype(jnp.float32) * scale
        p = jax.nn.softmax(jnp.concatenate([s_lat, s_ctx], axis=-1), axis=-1).astype(v.dtype)
        return (jnp.einsum('bhqk,bkhd->bqhd', p[..., :n_lat], v_band)
                + jnp.einsum('bhqk,bkhd->bqhd', p[..., n_lat:], vz))

    o = lax.map(row_block, (q_rows, jnp.arange(rows)))
    ox = jnp.moveaxis(o, 0, 1).reshape(B, S, H * dh) @ w_o
    oz = None
    if need_ctx:
        oz = context_attention(qz, kz, vz, scale).reshape(B, T, H * dh) @ w_o
    return ox, oz


def mla_attention(hx, hz, w_down, q_norm_g, w_q_up, kv_norm_g, w_kv_up, w_o, need_ctx):
    B, S, _ = hx.shape
    T = hz.shape[1]
    H = MLA_HEADS
    scale = (MLA_NOPE + MLA_ROPE) ** -0.5
    cos, sin = axial_rope_tables(S, MLA_ROPE)

    def project(h):
        n = h.shape[1]
        cq, ckv, k_pe = jnp.split(h @ w_down, [MLA_Q_LORA, MLA_Q_LORA + MLA_KV_LORA], axis=-1)
        q = (rmsnorm(cq, q_norm_g) @ w_q_up).reshape(B, n, H, MLA_NOPE + MLA_ROPE)
        kv = (rmsnorm(ckv, kv_norm_g) @ w_kv_up).reshape(B, n, H, MLA_NOPE + MLA_V)
        return q[..., :MLA_NOPE], q[..., MLA_NOPE:], kv[..., :MLA_NOPE], k_pe, kv[..., MLA_NOPE:]

    qn, qp, kn, kp, v = project(hx)
    qp = apply_rope(qp, cos[:, None, :], sin[:, None, :])
    kp = apply_rope(kp, cos, sin)
    qnz, qpz, knz, kpz, vz = project(hz)
    kn_all = jnp.concatenate([kn, knz], axis=1)
    kp_all = jnp.concatenate([kp, kpz], axis=1)
    v_all = jnp.concatenate([v, vz], axis=1)

    nb = S // MLA_BLOCK

    def to_blocks(t):
        return jnp.moveaxis(t.reshape(B, nb, MLA_BLOCK, *t.shape[2:]), 1, 0)

    def query_block(args):
        qn_b, qp_b = args
        s = (jnp.einsum('bqhd,bkhd->bhqk', qn_b, kn_all).astype(jnp.float32)
             + jnp.einsum('bqhd,bkd->bhqk', qp_b, kp_all).astype(jnp.float32)) * scale
        p = jax.nn.softmax(s, axis=-1).astype(v_all.dtype)
        return jnp.einsum('bhqk,bkhd->bqhd', p, v_all)

    o = lax.map(query_block, (to_blocks(qn), to_blocks(qp)))
    ox = jnp.moveaxis(o, 0, 1).reshape(B, S, H * MLA_V) @ w_o
    oz = None
    if need_ctx:
        s = (jnp.einsum('bqhd,bkhd->bhqk', qnz, knz).astype(jnp.float32)
             + jnp.einsum('bqhd,bkd->bhqk', qpz, kpz).astype(jnp.float32)) * scale
        p = jax.nn.softmax(s, axis=-1).astype(vz.dtype)
        oz = jnp.einsum('bhqk,bkhd->bqhd', p, vz).reshape(B, T, H * MLA_V) @ w_o
    return ox, oz


def window_gqa_sink(hx, hz, w_qkv, sink, w_o, need_ctx):
    B, S, _ = hx.shape
    T = hz.shape[1]
    H, KVH, dh = SWA_HEADS, SWA_KV_HEADS, SWA_HEAD_DIM
    G = H // KVH
    scale = dh ** -0.5

    def project(h):
        n = h.shape[1]
        q, k, v = jnp.split(h @ w_qkv, [H * dh, (H + KVH) * dh], axis=-1)
        return q.reshape(B, n, KVH, G, dh), k.reshape(B, n, KVH, dh), v.reshape(B, n, KVH, dh)

    q, k, v = project(hx)
    cos, sin = axial_rope_tables(S, dh)
    q = apply_rope(q, cos[:, None, None, :], sin[:, None, None, :])
    k = apply_rope(k, cos[:, None, :], sin[:, None, :])
    qz, kz, vz = project(hz)
    sink_f = sink.astype(jnp.float32).reshape(KVH, G)

    nb = S // SWA_BLOCK
    pad = ((0, 0), (SWA_BLOCK, SWA_BLOCK), (0, 0), (0, 0))

    def band(t):
        tp = jnp.pad(t, pad).reshape(B, nb + 2, SWA_BLOCK, KVH, dh)
        return jnp.concatenate([tp[:, :-2], tp[:, 1:-1], tp[:, 2:]], axis=2)

    k_band, v_band = band(k), band(v)
    q_blk = q.reshape(B, nb, SWA_BLOCK, KVH, G, dh)
    qi = jnp.arange(SWA_BLOCK)
    kj = jnp.arange(3 * SWA_BLOCK)
    rel = kj[None, :] - SWA_BLOCK - qi[:, None]
    kpos = jnp.arange(nb)[:, None] * SWA_BLOCK - SWA_BLOCK + kj[None, :]
    mask = (jnp.abs(rel) <= SWA_WINDOW)[None] & ((kpos >= 0) & (kpos < S))[:, None, :]
    s_lat = jnp.einsum('bnqkgd,bnjkd->bnkgqj', q_blk, k_band).astype(jnp.float32) * scale
    s_lat = jnp.where(mask[None, :, None, None], s_lat, NEG_INF)
    s_ctx = jnp.einsum('bnqkgd,bckd->bnkgqc', q_blk, kz).astype(jnp.float32) * scale
    p = softmax_with_sink(jnp.concatenate([s_lat, s_ctx], axis=-1),
                          sink_f[None, None, :, :, None, None]).astype(v.dtype)
    n_lat = 3 * SWA_BLOCK
    o = (jnp.einsum('bnkgqj,bnjkd->bnqkgd', p[..., :n_lat], v_band)
         + jnp.einsum('bnkgqc,bckd->bnqkgd', p[..., n_lat:], vz))
    ox = o.reshape(B, S, H * dh) @ w_o
    oz = None
    if need_ctx:
        s = jnp.einsum('bqkgd,bckd->bkgqc', qz, kz).astype(jnp.float32) * scale
        pz = softmax_with_sink(s, sink_f[None, :, :, None, None]).astype(vz.dtype)
        oz = jnp.einsum('bkgqc,bckd->bqkgd', pz, vz).reshape(B, T, H * dh) @ w_o
    return ox, oz


def setup_inputs(seed: int = 0) -> dict:
    key = jax.random.key(seed)
    ks = jax.random.split(key, 24)
    D, F = D_MODEL, D_FF
    n_a = len(range(0, DEPTH, N_MIXERS))
    n_b = len(range(1, DEPTH, N_MIXERS))
    n_c = len(range(2, DEPTH, N_MIXERS))

    def nrm(k, shape, s):
        return jax.random.normal(k, shape, jnp.float32) * s

    def gain(k, shape):
        return 1.0 + 0.02 * jax.random.normal(k, shape, jnp.float32)

    mla_down = MLA_Q_LORA + MLA_KV_LORA + MLA_ROPE
    swa_qkv = (SWA_HEADS + 2 * SWA_KV_HEADS) * SWA_HEAD_DIM
    return {
        'x': nrm(ks[0], (BATCH, SEQ, D), 1.0),
        'c': nrm(ks[1], (BATCH, D), 1.0),
        'ctx': nrm(ks[2], (BATCH, CTX_LEN, D), 1.0),
        'c_ctx': nrm(ks[3], (D,), 1.0),
        'w_mod': nrm(ks[4], (DEPTH, D, N_MOD * D), 0.5 * D ** -0.5),
        'b_mod': nrm(ks[5], (DEPTH, N_MOD * D), 0.01),
        'norm_g': gain(ks[6], (DEPTH, 3, D)),
        'ffn_w_in': nrm(ks[7], (DEPTH, 2, D, 2 * F), D ** -0.5),
        'ffn_w_out': nrm(ks[8], (DEPTH, 2, F, D), F ** -0.5),
        'na_w_qkv': nrm(ks[9], (n_a, D, 3 * NA_HEADS * NA_HEAD_DIM), D ** -0.5),
        'na_rpb': nrm(ks[10], (n_a, NA_HEADS, 2 * NA_KH - 1, 2 * NA_KW - 1), 0.1),
        'na_w_o': nrm(ks[11], (n_a, NA_HEADS * NA_HEAD_DIM, D), (NA_HEADS * NA_HEAD_DIM) ** -0.5),
        'mla_w_down': nrm(ks[12], (n_b, D, mla_down), D ** -0.5),
        'mla_q_norm_g': gain(ks[13], (n_b, MLA_Q_LORA)),
        'mla_w_q_up': nrm(ks[14], (n_b, MLA_Q_LORA, MLA_HEADS * (MLA_NOPE + MLA_ROPE)), MLA_Q_LORA ** -0.5),
        'mla_kv_norm_g': gain(ks[15], (n_b, MLA_KV_LORA)),
        'mla_w_kv_up': nrm(ks[16], (n_b, MLA_KV_LORA, MLA_HEADS * (MLA_NOPE + MLA_V)), MLA_KV_LORA ** -0.5),
        'mla_w_o': nrm(ks[17], (n_b, MLA_HEADS * MLA_V, D), (MLA_HEADS * MLA_V) ** -0.5),
        'swa_w_qkv': nrm(ks[18], (n_c, D, swa_qkv), D ** -0.5),
        'swa_sink': nrm(ks[19], (n_c, SWA_HEADS), 0.5),
        'swa_w_o': nrm(ks[20], (n_c, SWA_HEADS * SWA_HEAD_DIM, D), (SWA_HEADS * SWA_HEAD_DIM) ** -0.5),
        'final_norm_g': gain(ks[21], (D,)),
    }


def reference(x, c, ctx, c_ctx, w_mod, b_mod, norm_g, ffn_w_in, ffn_w_out,
              na_w_qkv, na_rpb, na_w_o,
              mla_w_down, mla_q_norm_g, mla_w_q_up, mla_kv_norm_g, mla_w_kv_up, mla_w_o,
              swa_w_qkv, swa_sink, swa_w_o, final_norm_g):
    z = ctx
    for li in range(DEPTH):
        need_ctx = li < DEPTH - 1
        j = li // N_MIXERS
        mx = adaln_params(c[:, None, :], w_mod[li], b_mod[li])
        mz = adaln_params(c_ctx[None, None, :], w_mod[li], b_mod[li])

        x = x + 0.5 * mx[2] * swiglu(modulate(x, norm_g[li, 0], mx[0], mx[1]), ffn_w_in[li, 0], ffn_w_out[li, 0])
        z = z + 0.5 * mz[2] * swiglu(modulate(z, norm_g[li, 0], mz[0], mz[1]), ffn_w_in[li, 0], ffn_w_out[li, 0])

        hx = modulate(x, norm_g[li, 1], mx[3], mx[4])
        hz = modulate(z, norm_g[li, 1], mz[3], mz[4])
        kind = li % N_MIXERS
        if kind == 0:
            ox, oz = neighbourhood_attention(hx, hz, na_w_qkv[j], na_rpb[j], na_w_o[j], need_ctx)
        elif kind == 1:
            ox, oz = mla_attention(hx, hz, mla_w_down[j], mla_q_norm_g[j], mla_w_q_up[j],
                                   mla_kv_norm_g[j], mla_w_kv_up[j], mla_w_o[j], need_ctx)
        else:
            ox, oz = window_gqa_sink(hx, hz, swa_w_qkv[j], swa_sink[j], swa_w_o[j], need_ctx)
        x = x + mx[5] * ox

        x = x + 0.5 * mx[8] * swiglu(modulate(x, norm_g[li, 2], mx[6], mx[7]), ffn_w_in[li, 1], ffn_w_out[li, 1])
        if need_ctx:
            z = z + mz[5] * oz
            z = z + 0.5 * mz[8] * swiglu(modulate(z, norm_g[li, 2], mz[6], mz[7]), ffn_w_in[li, 1], ffn_w_out[li, 1])
    return rmsnorm(x, final_norm_g)
```

```python
import functools
import math

import jax
import jax.numpy as jnp
from jax import lax
from jax.experimental import pallas as pl
from jax.experimental.pallas import tpu as pltpu

D_MODEL = 2048
DEPTH = 4
GRID_W = 64
N_MIXERS = 3
N_MOD = 9
RMS_EPS = 1e-6
ROPE_BASE = 10000.0
NEG_INF = -1e30

NA_HEADS = 16
NA_KH = 8
NA_KW = 16
HEAD_DIM = 128

MLA_HEADS = 16
MLA_Q_LORA = 512
MLA_KV_LORA = 512
MLA_NOPE = 128
MLA_ROPE = 64
MLA_V = 128

SWA_HEADS = 16
SWA_KV_HEADS = 4
SWA_WINDOW = 128

LANE = 128
ROW_TILE = 512
FFN_CHUNK = 512
VMEM_LIMIT = 56 * 1024 * 1024

F32 = jnp.float32
BF16 = jnp.bfloat16


def _cparams(*sem):
    return pltpu.CompilerParams(dimension_semantics=sem, vmem_limit_bytes=VMEM_LIMIT)


def _dot(a, b):
    return jnp.dot(a, b, preferred_element_type=F32)


def _dot_t(a, b):
    return lax.dot_general(a, b, (((1,), (1,)), ((), ())), preferred_element_type=F32)


def _sigmoid(x):
    return 1.0 / (1.0 + jnp.exp(-x))


def _rmsnorm_rows(x, g):
    ms = jnp.mean(x * x, axis=-1, keepdims=True)
    return x * lax.rsqrt(ms + RMS_EPS) * g


def _mod_kernel(c_ref, w_ref, b_ref, o_ref):
    c = c_ref[...]
    a = (c * _sigmoid(c)).astype(BF16)
    o_ref[0] = _dot(a, w_ref[0].astype(BF16)) + b_ref[0]


def mod_params(cond, w_mod, b_mod):
    depth, d, n = w_mod.shape
    g = cond.shape[0]
    tn = math.gcd(n, 1024)
    return pl.pallas_call(
        _mod_kernel,
        grid=(depth, n // tn),
        in_specs=[
            pl.BlockSpec((g, d), lambda l, j: (0, 0)),
            pl.BlockSpec((1, d, tn), lambda l, j: (l, 0, j)),
            pl.BlockSpec((1, 1, tn), lambda l, j: (l, 0, j)),
        ],
        out_specs=pl.BlockSpec((1, g, tn), lambda l, j: (l, 0, j)),
        out_shape=jax.ShapeDtypeStruct((depth, g, n), F32),
        compiler_params=_cparams("parallel", "parallel"),
        name="mod_params",
    )(cond, w_mod, b_mod.reshape(depth, 1, n))


def _ffn_kernel(*refs, k0, final_norm):
    if final_norm:
        x_ref, g_ref, mod_ref, wg_ref, wu_ref, wo_ref, gf_ref, o_ref, h_ref, acc_ref = refs
    else:
        x_ref, g_ref, mod_ref, wg_ref, wu_ref, wo_ref, o_ref, h_ref, acc_ref = refs
    j = pl.program_id(1)

    @pl.when(j == 0)
    def _():
        y = _rmsnorm_rows(x_ref[...], g_ref[...])
        y = y * (1.0 + mod_ref[0, k0 + 1:k0 + 2, :]) + mod_ref[0, k0:k0 + 1, :]
        h_ref[...] = y.astype(BF16)
        acc_ref[...] = jnp.zeros_like(acc_ref)

    h = h_ref[...]
    gate = _dot(h, wg_ref[...])
    up = _dot(h, wu_ref[...])
    a = (gate * _sigmoid(gate) * up).astype(BF16)
    acc_ref[...] += _dot(a, wo_ref[...])

    @pl.when(j == pl.num_programs(1) - 1)
    def _():
        y = x_ref[...] + (0.5 * mod_ref[0, k0 + 2:k0 + 3, :]) * acc_ref[...]
        if final_norm:
            y = _rmsnorm_rows(y, gf_ref[...])
        o_ref[...] = y


def ffn_half_step(x, gain, mods, w_in, w_out, *, k0, n_tiles, tiles_per_group,
                  n_groups, final_gain=None, tm=ROW_TILE, tf=FFN_CHUNK):
    d = x.shape[1]
    f = w_out.shape[0]
    nf = f // tf
    grp = lambda i: jnp.minimum(i // tiles_per_group, n_groups - 1)
    in_specs = [
        pl.BlockSpec((tm, d), lambda i, j: (i, 0)),
        pl.BlockSpec((1, d), lambda i, j: (0, 0)),
        pl.BlockSpec((1, N_MOD, d), lambda i, j: (grp(i), 0, 0)),
        pl.BlockSpec((d, tf), lambda i, j: (0, j)),
        pl.BlockSpec((d, tf), lambda i, j: (0, j + nf)),
        pl.BlockSpec((tf, d), lambda i, j: (j, 0)),
    ]
    args = [x, gain.reshape(1, d), mods, w_in, w_in, w_out]
    if final_gain is not None:
        in_specs.append(pl.BlockSpec((1, d), lambda i, j: (0, 0)))
        args.append(final_gain.reshape(1, d))
    return pl.pallas_call(
        functools.partial(_ffn_kernel, k0=k0, final_norm=final_gain is not None),
        grid=(n_tiles, nf),
        in_specs=in_specs,
        out_specs=pl.BlockSpec((tm, d), lambda i, j: (i, 0)),
        out_shape=jax.ShapeDtypeStruct((n_tiles * tm, d), F32),
        scratch_shapes=[pltpu.VMEM((tm, d), BF16), pltpu.VMEM((tm, d), F32)],
        compiler_params=_cparams("parallel", "arbitrary"),
        name="ffn_half_step",
    )(*args)


def _norm_linear_kernel(*refs, mod_idx, rope, head_major, n_chunks):
    it = iter(refs)
    x_ref, g_ref = next(it), next(it)
    mod_ref = next(it) if mod_idx is not None else None
    w_ref = next(it)
    if rope is None:
        cs_ref = next(it)
    else:
        tab_refs = [next(it) for _ in range(len(rope) + 1)]
    o_ref, h_ref = next(it), next(it)

    @pl.when(pl.program_id(1) == 0)
    def _():
        y = _rmsnorm_rows(x_ref[...], g_ref[...])
        if mod_idx is not None:
            k_shift, k_scale = mod_idx
            y = y * (1.0 + mod_ref[0, k_scale:k_scale + 1, :]) + mod_ref[0, k_shift:k_shift + 1, :]
        h_ref[...] = y.astype(BF16)

    acc = _dot(h_ref[...], w_ref[...])
    if rope is None:
        acc = acc * cs_ref[...]
    for c in range(n_chunks):
        a = acc[:, c * LANE:(c + 1) * LANE]
        if rope is not None:
            out = a * tab_refs[0][0]
            for shift, tab in zip(rope, tab_refs[1:]):
                out = out + pltpu.roll(a, shift, 1) * tab[0]
            a = out
        if head_major:
            o_ref[c] = a.astype(o_ref.dtype)
        else:
            o_ref[:, c * LANE:(c + 1) * LANE] = a.astype(o_ref.dtype)


def norm_linear(x, x_col_block, k, gain, w, *, n_tiles, tiles_per_group, n_groups,
                mods=None, mod_idx=None, col_scale=None, rope=None, tables=None,
                set_thresholds=(), pos_tiles=None, head_major=True, out_dtype=BF16,
                tm=ROW_TILE, tn=512):
    n = w.shape[1]
    nj = n // tn
    n_chunks = tn // LANE
    grp = lambda i: jnp.minimum(i // tiles_per_group, n_groups - 1)
    in_specs = [
        pl.BlockSpec((tm, k), lambda i, j: (i, x_col_block)),
        pl.BlockSpec((1, k), lambda i, j: (0, 0)),
    ]
    args = [x, gain.reshape(1, k)]
    if mod_idx is not None:
        in_specs.append(pl.BlockSpec((1, N_MOD, k), lambda i, j: (grp(i), 0, 0)))
        args.append(mods)
    in_specs.append(pl.BlockSpec((k, tn), lambda i, j: (0, j)))
    args.append(w)
    if rope is None:
        if col_scale is None:
            col_scale = jnp.ones((n,), F32)
        in_specs.append(pl.BlockSpec((1, tn), lambda i, j: (0, j)))
        args.append(col_scale.reshape(1, n))
    else:
        n_lat_tiles = tiles_per_group * (n_groups - 1)

        def tab_map(i, j):
            s = sum((j >= t).astype(jnp.int32) for t in set_thresholds) if set_thresholds else 0
            p = jnp.where(i < n_lat_tiles, i % pos_tiles, pos_tiles)
            return (s, p, 0)

        for t in tables:
            in_specs.append(pl.BlockSpec((1, tm, LANE), tab_map))
            args.append(t)
    rows = n_tiles * tm
    if head_major:
        out_spec = pl.BlockSpec((n_chunks, tm, LANE), lambda i, j: (j, i, 0))
        out_shape = jax.ShapeDtypeStruct((n // LANE, rows, LANE), out_dtype)
    else:
        out_spec = pl.BlockSpec((tm, tn), lambda i, j: (i, j))
        out_shape = jax.ShapeDtypeStruct((rows, n), out_dtype)
    return pl.pallas_call(
        functools.partial(_norm_linear_kernel, mod_idx=mod_idx, rope=rope,
                          head_major=head_major, n_chunks=n_chunks),
        grid=(n_tiles, nj),
        in_specs=in_specs,
        out_specs=out_spec,
        out_shape=out_shape,
        scratch_shapes=[pltpu.VMEM((tm, k), BF16)],
        compiler_params=_cparams("parallel", "arbitrary"),
        name="norm_linear",
    )(*args)


def _linear_residual_kernel(a_ref, w_ref, x_ref, mod_ref, o_ref, *, k_gate):
    acc = _dot(a_ref[...], w_ref[...])
    o_ref[...] = x_ref[...] + mod_ref[0, k_gate:k_gate + 1, :] * acc


def linear_residual(a, w, x, mods, *, k_gate, n_tiles, tiles_per_group, n_groups,
                    tm=ROW_TILE, tn=1024):
    kdim, n = w.shape
    tn = math.gcd(n, tn)
    grp = lambda i: jnp.minimum(i // tiles_per_group, n_groups - 1)
    return pl.pallas_call(
        functools.partial(_linear_residual_kernel, k_gate=k_gate),
        grid=(n_tiles, n // tn),
        in_specs=[
            pl.BlockSpec((tm, kdim), lambda i, j: (i, 0)),
            pl.BlockSpec((kdim, tn), lambda i, j: (0, j)),
            pl.BlockSpec((tm, tn), lambda i, j: (i, j)),
            pl.BlockSpec((1, N_MOD, tn), lambda i, j: (grp(i), 0, j)),
        ],
        out_specs=pl.BlockSpec((tm, tn), lambda i, j: (i, j)),
        out_shape=jax.ShapeDtypeStruct((n_tiles * tm, n), F32),
        compiler_params=_cparams("parallel", "arbitrary"),
        name="linear_residual",
    )(a, w, x, mods)


NA_QROWS = 4
NA_QTOK = NA_QROWS * GRID_W
NA_BAND_BLOCKS = 3
NA_PAIRS = 2 * NA_KH


def _na_bias_kernel(rpb_ref, o_ref):
    h = pl.program_id(0)
    cq = lax.broadcasted_iota(jnp.int32, (GRID_W, 2 * GRID_W), 0)
    lane = lax.broadcasted_iota(jnp.int32, (GRID_W, 2 * GRID_W), 1)
    second = lane >= GRID_W
    ck = jnp.where(second, lane - GRID_W, lane)
    n_dc = 2 * NA_KW - 1
    code = ck - cq + (NA_KW - 1) + jnp.where(second, n_dc, 0)
    start = jnp.clip(cq - NA_KW // 2, 0, GRID_W - NA_KW)
    in_win = (ck >= start) & (ck < start + NA_KW)
    for p in range(NA_PAIRS):
        tile = jnp.zeros((GRID_W, 2 * GRID_W), F32)
        for half in range(2):
            dr = p - NA_KH + half
            if not (-(NA_KH - 1) <= dr <= NA_KH - 1):
                continue
            for dc in range(n_dc):
                val = rpb_ref[h, dr + NA_KH - 1, dc]
                tile = jnp.where(code == half * n_dc + dc, val, tile)
        o_ref[0, p] = jnp.where(in_win, tile, NEG_INF)


def na_bias_table(rpb):
    heads = rpb.shape[0]
    return pl.pallas_call(
        _na_bias_kernel,
        grid=(heads,),
        in_specs=[pl.BlockSpec(memory_space=pltpu.SMEM)],
        out_specs=pl.BlockSpec((1, NA_PAIRS, GRID_W, 2 * GRID_W), lambda h: (h, 0, 0, 0)),
        out_shape=jax.ShapeDtypeStruct((heads, NA_PAIRS, GRID_W, 2 * GRID_W), F32),
        compiler_params=_cparams("parallel"),
        name="na_bias_table",
    )(rpb)


def _na_kernel(q_ref, k0_ref, k1_ref, k2_ref, v0_ref, v1_ref, v2_ref, kz_ref, vz_ref,
               t2_ref, o_ref, vm_ref, ob_ref, *, rows):
    qb = pl.program_id(1)
    r0 = NA_QROWS * qb
    u0 = jnp.clip(r0 - NA_KH // 2, 0, rows - NA_BAND_BLOCKS * NA_QROWS)
    k_refs = (k0_ref, k1_ref, k2_ref)
    v_refs = (v0_ref, v1_ref, v2_ref)

    q_row = r0 + (lax.broadcasted_iota(jnp.int32, (NA_QTOK, NA_QTOK), 0) >> 6)
    k_off = lax.broadcasted_iota(jnp.int32, (NA_QTOK, NA_QTOK), 1) >> 6
    rs = jnp.clip(q_row - NA_KH // 2, 0, rows - NA_KH)
    for i in range(NA_BAND_BLOCKS):
        rk = u0 + NA_QROWS * i + k_off
        vm_ref[i] = jnp.where((rk >= rs) & (rk < rs + NA_KH), 0.0, NEG_INF)

    def head_body(h, carry):
        qh = q_ref[h]
        s = []
        for i in range(NA_BAND_BLOCKS):
            si = _dot_t(qh, k_refs[i][h])
            bias_rows = []
            for qi in range(NA_QROWS):
                pair = []
                for c in range(2):
                    dr0 = u0 + NA_QROWS * i + 2 * c - (r0 + qi)
                    pair.append(t2_ref[h, jnp.clip(dr0 + NA_KH, 0, NA_PAIRS - 1)])
                bias_rows.append(jnp.concatenate(pair, axis=1))
            s.append(si + jnp.concatenate(bias_rows, axis=0) + vm_ref[i])
        s.append(_dot_t(qh, kz_ref[h]))
        m = s[0].max(axis=-1, keepdims=True)
        for si in s[1:]:
            m = jnp.maximum(m, si.max(axis=-1, keepdims=True))
        p = [jnp.exp(si - m) for si in s]
        l = p[0].sum(axis=-1, keepdims=True)
        for pi in p[1:]:
            l = l + pi.sum(axis=-1, keepdims=True)
        o = _dot(p[-1].astype(BF16), vz_ref[h])
        for i in range(NA_BAND_BLOCKS):
            o = o + _dot(p[i].astype(BF16), v_refs[i][h])
        ob_ref[h] = (o * (1.0 / l)).astype(BF16)
        return carry

    lax.fori_loop(0, NA_HEADS, head_body, 0)
    for h in range(NA_HEADS):
        o_ref[:, h * HEAD_DIM:(h + 1) * HEAD_DIM] = ob_ref[h]


def na_attention(qkv, t2, *, batch, seq, ctx_len):
    rows = seq // GRID_W
    nqb = seq // NA_QTOK
    z0 = batch * seq // ctx_len
    assert ctx_len == NA_QTOK and rows >= NA_BAND_BLOCKS * NA_QROWS
    blk = (NA_HEADS, NA_QTOK, HEAD_DIM)

    def band(i):
        return lambda b, q: (0, b * nqb + jnp.clip(q - 1, 0, nqb - NA_BAND_BLOCKS) + i, 0)

    def shifted(fn, part):
        return lambda b, q: (part,) + fn(b, q)[1:]

    in_specs = [pl.BlockSpec(blk, lambda b, q: (0, b * nqb + q, 0))]
    in_specs += [pl.BlockSpec(blk, shifted(band(i), 1)) for i in range(NA_BAND_BLOCKS)]
    in_specs += [pl.BlockSpec(blk, shifted(band(i), 2)) for i in range(NA_BAND_BLOCKS)]
    in_specs += [pl.BlockSpec(blk, lambda b, q: (1, z0 + b, 0)),
                 pl.BlockSpec(blk, lambda b, q: (2, z0 + b, 0)),
                 pl.BlockSpec(t2.shape, lambda b, q: (0, 0, 0, 0))]
    return pl.pallas_call(
        functools.partial(_na_kernel, rows=rows),
        grid=(batch, nqb),
        in_specs=in_specs,
        out_specs=pl.BlockSpec((NA_QTOK, NA_HEADS * HEAD_DIM), lambda b, q: (b * nqb + q, 0)),
        out_shape=jax.ShapeDtypeStruct((batch * seq, NA_HEADS * HEAD_DIM), BF16),
        scratch_shapes=[pltpu.VMEM((NA_BAND_BLOCKS, NA_QTOK, NA_QTOK), F32),
                        pltpu.VMEM(blk, BF16)],
        compiler_params=_cparams("parallel", "arbitrary"),
        name="na_attention",
    )(*([qkv] * 9), t2)


MLA_TQ = 512
MLA_TK = 512


def _mla_kernel(qn_ref, qp_ref, kn_ref, kp_ref, v_ref, knz_ref, kpz_ref, vz_ref, o_ref, *, seq):
    q = jnp.concatenate([qn_ref[0], qp_ref[0]], axis=1)
    tq = q.shape[0]

    def step(k, v, m, l, acc):
        s = _dot_t(q, k)
        m_new = jnp.maximum(m, s.max(axis=-1, keepdims=True))
        alpha = jnp.exp(m - m_new)
        p = jnp.exp(s - m_new)
        l = alpha * l + p.sum(axis=-1, keepdims=True)
        acc = alpha * acc + _dot(p.astype(BF16), v)
        return m_new, l, acc

    def body(kb, carry):
        off = pl.multiple_of(kb * MLA_TK, MLA_TK)
        k = jnp.concatenate([kn_ref[0, pl.ds(off, MLA_TK), :], kp_ref[0, pl.ds(off, MLA_TK), :]], axis=1)
        return step(k, v_ref[0, pl.ds(off, MLA_TK), :], *carry)

    init = (jnp.full((tq, 1), NEG_INF, F32), jnp.zeros((tq, 1), F32), jnp.zeros((tq, MLA_V), F32))
    carry = lax.fori_loop(0, seq // MLA_TK, body, init)
    kz = jnp.concatenate([knz_ref[0], kpz_ref[0]], axis=1)
    m, l, acc = step(kz, vz_ref[0], *carry)
    o_ref[...] = (acc * (1.0 / l)).astype(BF16)


def mla_attention(qq, kv, kp, *, batch, seq, ctx_len):
    h_ = MLA_HEADS
    nqb = seq // MLA_TQ
    z0 = batch * seq // ctx_len
    qblk = (1, MLA_TQ, LANE)
    kblk = (1, seq, LANE)
    zblk = (1, ctx_len, LANE)
    in_specs = [
        pl.BlockSpec(qblk, lambda b, h, q: (h, b * nqb + q, 0)),
        pl.BlockSpec(qblk, lambda b, h, q: (h_ + h, b * nqb + q, 0)),
        pl.BlockSpec(kblk, lambda b, h, q: (h, b, 0)),
        pl.BlockSpec(kblk, lambda b, h, q: (0, b, 0)),
        pl.BlockSpec(kblk, lambda b, h, q: (h_ + h, b, 0)),
        pl.BlockSpec(zblk, lambda b, h, q: (h, z0 + b, 0)),
        pl.BlockSpec(zblk, lambda b, h, q: (0, z0 + b, 0)),
        pl.BlockSpec(zblk, lambda b, h, q: (h_ + h, z0 + b, 0)),
    ]
    return pl.pallas_call(
        functools.partial(_mla_kernel, seq=seq),
        grid=(batch, h_, nqb),
        in_specs=in_specs,
        out_specs=pl.BlockSpec((MLA_TQ, MLA_V), lambda b, h, q: (b * nqb + q, h)),
        out_shape=jax.ShapeDtypeStruct((batch * seq, h_ * MLA_V), BF16),
        compiler_params=_cparams("parallel", "parallel", "arbitrary"),
        name="mla_attention",
    )(qq, qq, kv, kp, kv, kv, kp, kv)


SWA_TQ = SWA_WINDOW


def _swa_kernel(sink_ref, q_ref, k0_ref, k1_ref, k2_ref, v0_ref, v1_ref, v2_ref,
                kz_ref, vz_ref, o_ref, *, n_blocks):
    qb = pl.program_id(1)
    g = SWA_HEADS // SWA_KV_HEADS
    m_rows = g * SWA_TQ
    row = lax.broadcasted_iota(jnp.int32, (m_rows, SWA_TQ), 0) & (SWA_TQ - 1)
    lane = lax.broadcasted_iota(jnp.int32, (m_rows, SWA_TQ), 1)
    mask_prev = (lane >= row) & (qb >= 1)
    mask_next = (lane <= row) & (qb <= n_blocks - 2)
    grp_id = lax.broadcasted_iota(jnp.int32, (m_rows, 1), 0) >> 7
    for kvh in range(SWA_KV_HEADS):
        q4 = q_ref[kvh * g:(kvh + 1) * g].reshape(m_rows, HEAD_DIM)
        sink = jnp.zeros((m_rows, 1), F32)
        for gi in range(g):
            sink = jnp.where(grp_id == gi, sink_ref[kvh * g + gi], sink)
        s = [jnp.where(mask_prev, _dot_t(q4, k0_ref[kvh]), NEG_INF),
             _dot_t(q4, k1_ref[kvh]),
             jnp.where(mask_next, _dot_t(q4, k2_ref[kvh]), NEG_INF),
             _dot_t(q4, kz_ref[kvh])]
        m = sink
        for si in s:
            m = jnp.maximum(m, si.max(axis=-1, keepdims=True))
        p = [jnp.exp(si - m) for si in s]
        l = jnp.exp(sink - m)
        for pi in p:
            l = l + pi.sum(axis=-1, keepdims=True)
        vs = (v0_ref, v1_ref, v2_ref, vz_ref)
        o = _dot(p[0].astype(BF16), vs[0][kvh])
        for pi, vr in zip(p[1:], vs[1:]):
            o = o + _dot(pi.astype(BF16), vr[kvh])
        o = (o * (1.0 / l)).astype(BF16)
        for gi in range(g):
            hq = kvh * g + gi
            o_ref[:, hq * HEAD_DIM:(hq + 1) * HEAD_DIM] = o[gi * SWA_TQ:(gi + 1) * SWA_TQ]


def swa_attention(qkv, sink, *, batch, seq, ctx_len):
    nb = seq // SWA_TQ
    kvh = SWA_KV_HEADS
    qpart = SWA_HEADS // kvh
    z0 = batch * seq // ctx_len
    kblk = (kvh, SWA_TQ, HEAD_DIM)
    zblk = (kvh, ctx_len, HEAD_DIM)

    def kmap(part, d):
        return lambda b, q: (part, b * nb + jnp.clip(q + d, 0, nb - 1), 0)

    in_specs = [pl.BlockSpec(memory_space=pltpu.SMEM),
                pl.BlockSpec((SWA_HEADS, SWA_TQ, HEAD_DIM), lambda b, q: (0, b * nb + q, 0))]
    in_specs += [pl.BlockSpec(kblk, kmap(qpart, d)) for d in (-1, 0, 1)]
    in_specs += [pl.BlockSpec(kblk, kmap(qpart + 1, d)) for d in (-1, 0, 1)]
    in_specs += [pl.BlockSpec(zblk, lambda b, q: (qpart, z0 + b, 0)),
                 pl.BlockSpec(zblk, lambda b, q: (qpart + 1, z0 + b, 0))]
    return pl.pallas_call(
        functools.partial(_swa_kernel, n_blocks=nb),
        grid=(batch, nb),
        in_specs=in_specs,
        out_specs=pl.BlockSpec((SWA_TQ, SWA_HEADS * HEAD_DIM), lambda b, q: (b * nb + q, 0)),
        out_shape=jax.ShapeDtypeStruct((batch * seq, SWA_HEADS * HEAD_DIM), BF16),
        compiler_params=_cparams("parallel", "arbitrary"),
        name="swa_attention",
    )(sink, *([qkv] * 9))


def _ctx_kernel(*refs, two_part, use_sink):
    it = iter(refs)
    sink_ref = next(it) if use_sink else None
    if two_part:
        q = jnp.concatenate([next(it)[0], next(it)[0]], axis=1)
        k = jnp.concatenate([next(it)[0], next(it)[0]], axis=1)
    else:
        q = next(it)[0]
        k = next(it)[0]
    v_ref, o_ref = next(it), next(it)
    s = _dot_t(q, k)
    m = s.max(axis=-1, keepdims=True)
    if use_sink:
        sink = sink_ref[pl.program_id(1)]
        m = jnp.maximum(m, sink)
    p = jnp.exp(s - m)
    l = p.sum(axis=-1, keepdims=True)
    if use_sink:
        l = l + jnp.exp(sink - m)
    o_ref[...] = (_dot(p.astype(BF16), v_ref[0]) * (1.0 / l)).astype(BF16)


def ctx_attention(q_parts, k_parts, v_part, *, batch, ctx_len, z0, n_heads, sink=None):
    blk = (1, ctx_len, LANE)
    in_specs, args = [], []
    if sink is not None:
        in_specs.append(pl.BlockSpec(memory_space=pltpu.SMEM))
        args.append(sink)
    for arr, head_fn in (*q_parts, *k_parts, v_part):
        in_specs.append(pl.BlockSpec(blk, functools.partial(
            lambda b, h, head_fn: (head_fn(h), z0 + b, 0), head_fn=head_fn)))
        args.append(arr)
    return pl.pallas_call(
        functools.partial(_ctx_kernel, two_part=len(q_parts) == 2, use_sink=sink is not None),
        grid=(batch, n_heads),
        in_specs=in_specs,
        out_specs=pl.BlockSpec((ctx_len, LANE), lambda b, h: (b, h)),
        out_shape=jax.ShapeDtypeStruct((batch * ctx_len, n_heads * LANE), BF16),
        compiler_params=_cparams("parallel", "parallel"),
        name="ctx_attention",
    )(*args)


def _rope_angles(n_tokens, rot_dim):
    t = jnp.arange(n_tokens)
    row = (t // GRID_W).astype(F32)
    col = (t % GRID_W).astype(F32)
    n_freq = rot_dim // 4
    inv_freq = ROPE_BASE ** (-jnp.arange(n_freq, dtype=F32) / n_freq)
    ang = jnp.concatenate([row[:, None] * inv_freq, col[:, None] * inv_freq], axis=-1)
    return jnp.cos(ang), jnp.sin(ang)


def _with_identity_tile(tab, fill, tm):
    return jnp.concatenate([tab, jnp.broadcast_to(fill, (tm, LANE))], axis=0)


def swa_rope_tables(seq, scale, tm):
    cos, sin = _rope_angles(seq, HEAD_DIM)
    c = jnp.concatenate([cos, cos], axis=-1)
    s = jnp.concatenate([-sin, sin], axis=-1)
    one = jnp.ones((LANE,), F32)
    zero = jnp.zeros((LANE,), F32)
    ident_c = _with_identity_tile(jnp.ones_like(c), one, tm)
    ident_s = _with_identity_tile(jnp.zeros_like(s), zero, tm)
    c_sets = jnp.stack([_with_identity_tile(c * scale, one * scale, tm), _with_identity_tile(c, one, tm), ident_c])
    s_sets = jnp.stack([_with_identity_tile(s * scale, zero, tm), _with_identity_tile(s, zero, tm), ident_s])
    return c_sets, s_sets


def mla_rope_tables(seq, scale, tm):
    cos, sin = _rope_angles(seq, MLA_ROPE)
    half = MLA_ROPE // 2
    z = jnp.zeros((seq, half), F32)
    c = jnp.concatenate([cos, cos, z, z], axis=-1)
    s1 = jnp.concatenate([-sin, z, z, z], axis=-1)
    s2 = jnp.concatenate([z, sin, z, z], axis=-1)
    one = jnp.ones((LANE,), F32)
    zero = jnp.zeros((LANE,), F32)
    wt = _with_identity_tile
    c_sets = jnp.stack([wt(jnp.ones_like(c) * scale, one * scale, tm), wt(c * scale, one * scale, tm), wt(c, one, tm)])
    s1_sets = jnp.stack([wt(jnp.zeros_like(c), zero, tm), wt(s1 * scale, zero, tm), wt(s1, zero, tm)])
    s2_sets = jnp.stack([wt(jnp.zeros_like(c), zero, tm), wt(s2 * scale, zero, tm), wt(s2, zero, tm)])
    return c_sets, s1_sets, s2_sets


def kernel(x, c, ctx, c_ctx, w_mod, b_mod, norm_g, ffn_w_in, ffn_w_out, na_w_qkv, na_rpb, na_w_o,
           mla_w_down, mla_q_norm_g, mla_w_q_up, mla_kv_norm_g, mla_w_kv_up, mla_w_o,
           swa_w_qkv, swa_sink, swa_w_o, final_norm_g):
    batch, seq, d = x.shape
    ctx_len = ctx.shape[1]
    depth = w_mod.shape[0]
    tm = ROW_TILE
    nx, nz = batch * seq, batch * ctx_len
    tiles_per_group = seq // tm
    n_groups = batch + 1
    x_tiles = nx // tm
    all_tiles = (nx + nz) // tm
    z0 = nx // ctx_len
    assert seq % tm == 0 and nz % tm == 0 and nx % ctx_len == 0

    n_cond = 8
    cond = jnp.concatenate([c, c_ctx[None, :], jnp.zeros((n_cond - batch - 1, d), F32)], axis=0)
    mods_all = mod_params(cond, w_mod, b_mod).reshape(depth, n_cond, N_MOD, d)

    stream = jnp.concatenate([x.reshape(nx, d), ctx.reshape(nz, d)], axis=0)
    common = dict(tiles_per_group=tiles_per_group, n_groups=n_groups)
    qk_scale = HEAD_DIM ** -0.5
    mla_scale = (MLA_NOPE + MLA_ROPE) ** -0.5

    for li in range(depth):
        last = li == depth - 1
        j = li // N_MIXERS
        kind = li % N_MIXERS
        mods = mods_all[li]
        w_in = ffn_w_in[li].astype(BF16)
        w_out = ffn_w_out[li].astype(BF16)

        stream = ffn_half_step(stream, norm_g[li, 0], mods, w_in[0], w_out[0], k0=0,
                               n_tiles=all_tiles, **common)

        proj = dict(n_tiles=all_tiles, mods=mods, mod_idx=(3, 4), **common)
        if kind == 0:
            col_scale = jnp.concatenate([jnp.full((NA_HEADS * HEAD_DIM,), qk_scale, F32),
                                         jnp.ones((2 * NA_HEADS * HEAD_DIM,), F32)])
            qkv = norm_linear(stream, 0, d, norm_g[li, 1], na_w_qkv[j].astype(BF16),
                              col_scale=col_scale, tn=1024, **proj)
            t2 = na_bias_table(na_rpb[j])
            ox = na_attention(qkv, t2, batch=batch, seq=seq, ctx_len=ctx_len)
            if not last:
                hd = lambda part: (qkv, lambda h: part * NA_HEADS + h)
                oz = ctx_attention([hd(0)], [hd(1)], hd(2), batch=batch, ctx_len=ctx_len,
                                   z0=z0, n_heads=NA_HEADS)
            w_o = na_w_o[j]
        elif kind == 1:
            w_down = mla_w_down[j]
            lora = MLA_Q_LORA + MLA_KV_LORA
            cqkv = norm_linear(stream, 0, d, norm_g[li, 1], w_down[:, :lora].astype(BF16),
                               head_major=False, out_dtype=F32, tn=lora, **proj)
            c_t, s1_t, s2_t = mla_rope_tables(seq, mla_scale, tm)
            w_kpe = jnp.pad(w_down[:, lora:], ((0, 0), (0, LANE - MLA_ROPE))).astype(BF16)
            kp = norm_linear(stream, 0, d, norm_g[li, 1], w_kpe, rope=(96, 32),
                             tables=(c_t[2:], s1_t[2:], s2_t[2:]), pos_tiles=tiles_per_group,
                             tn=LANE, **proj)
            wq = mla_w_q_up[j].reshape(MLA_Q_LORA, MLA_HEADS, MLA_NOPE + MLA_ROPE)
            wq_n = wq[:, :, :MLA_NOPE].reshape(MLA_Q_LORA, MLA_HEADS * MLA_NOPE)
            wq_p = jnp.pad(wq[:, :, MLA_NOPE:], ((0, 0), (0, 0), (0, LANE - MLA_ROPE)))
            wq_all = jnp.concatenate([wq_n, wq_p.reshape(MLA_Q_LORA, MLA_HEADS * LANE)], axis=1).astype(BF16)
            up = dict(n_tiles=all_tiles, **common)
            qq = norm_linear(cqkv, 0, MLA_Q_LORA, mla_q_norm_g[j], wq_all, rope=(96, 32),
                             tables=(c_t[:2], s1_t[:2], s2_t[:2]),
                             set_thresholds=(MLA_HEADS * MLA_NOPE // 512,),
                             pos_tiles=tiles_per_group, tn=512, **up)
            wkv = mla_w_kv_up[j].reshape(MLA_KV_LORA, MLA_HEADS, MLA_NOPE + MLA_V)
            wkv_all = jnp.concatenate([wkv[:, :, :MLA_NOPE].reshape(MLA_KV_LORA, -1),
                                       wkv[:, :, MLA_NOPE:].reshape(MLA_KV_LORA, -1)], axis=1).astype(BF16)
            kv = norm_linear(cqkv, 1, MLA_KV_LORA, mla_kv_norm_g[j], wkv_all, tn=1024, **up)
            ox = mla_attention(qq, kv, kp, batch=batch, seq=seq, ctx_len=ctx_len)
            if not last:
                oz = ctx_attention([(qq, lambda h: h), (qq, lambda h: MLA_HEADS + h)],
                                   [(kv, lambda h: h), (kp, lambda h: 0)],
                                   (kv, lambda h: MLA_HEADS + h),
                                   batch=batch, ctx_len=ctx_len, z0=z0, n_heads=MLA_HEADS)
            w_o = mla_w_o[j]
        else:
            c_t, s_t = swa_rope_tables(seq, qk_scale, tm)
            n_q_blocks = SWA_HEADS * HEAD_DIM // 512
            qkv = norm_linear(stream, 0, d, norm_g[li, 1], swa_w_qkv[j].astype(BF16), rope=(64,),
                              tables=(c_t, s_t), set_thresholds=(n_q_blocks, n_q_blocks + 1),
                              pos_tiles=tiles_per_group, tn=512, **proj)
            ox = swa_attention(qkv, swa_sink[j], batch=batch, seq=seq, ctx_len=ctx_len)
            if not last:
                g = SWA_HEADS // SWA_KV_HEADS
                oz = ctx_attention([(qkv, lambda h: h)], [(qkv, lambda h: SWA_HEADS + h // g)],
                                   (qkv, lambda h: SWA_HEADS + SWA_KV_HEADS + h // g),
                                   batch=batch, ctx_len=ctx_len, z0=z0, n_heads=SWA_HEADS,
                                   sink=swa_sink[j])
            w_o = swa_w_o[j]

        n_tiles = x_tiles if last else all_tiles
        attn = ox if last else jnp.concatenate([ox, oz], axis=0)
        stream = linear_residual(attn, w_o.astype(BF16), stream, mods, k_gate=5,
                                 n_tiles=n_tiles, **common)
        stream = ffn_half_step(stream, norm_g[li, 2], mods, w_in[1], w_out[1], k0=6,
                               n_tiles=n_tiles, final_gain=final_norm_g if last else None, **common)

    return stream.reshape(batch, seq, d)
```

```python
import functools
import math

import jax
import jax.numpy as jnp
from jax import lax
from jax.experimental import pallas as pl
from jax.experimental.pallas import tpu as pltpu

DEPTH = 4
GRID_W = 64
N_MIXERS = 3
N_MOD = 9
RMS_EPS = 1e-6
ROPE_BASE = 10000.0
NEG_INF = -1e30
LOG2E = math.log2(math.e)

NA_HEADS = 16
NA_KH = 8
NA_KW = 16
HEAD_DIM = 128

MLA_HEADS = 16
MLA_Q_LORA = 512
MLA_KV_LORA = 512
MLA_NOPE = 128
MLA_ROPE = 64
MLA_V = 128

SWA_HEADS = 16
SWA_KV_HEADS = 4
SWA_WINDOW = 128

LANE = 128
FFN_ROW_TILE = 512
PROJ_ROW_TILE = 1024
FFN_CHUNK = 512
VMEM_LIMIT = 56 * 1024 * 1024

F32 = jnp.float32
BF16 = jnp.bfloat16


def _cparams(*sem):
    return pltpu.CompilerParams(dimension_semantics=sem, vmem_limit_bytes=VMEM_LIMIT)


def _dot(a, b):
    return jnp.dot(a, b, preferred_element_type=F32)


def _dot_t(a, b):
    return lax.dot_general(a, b, (((1,), (1,)), ((), ())), preferred_element_type=F32)


def _sigmoid(x):
    return 1.0 / (1.0 + jnp.exp(-x))


def _rmsnorm_rows(x, g):
    ms = jnp.mean(x * x, axis=-1, keepdims=True)
    return x * lax.rsqrt(ms + RMS_EPS) * g


def _modulated_norm(x, g_ref, mod_ref, k_shift, k_scale):
    r = lax.rsqrt(jnp.mean(x * x, axis=-1, keepdims=True) + RMS_EPS)
    gs = g_ref[...] * (1.0 + mod_ref[0, k_scale:k_scale + 1, :])
    return (x * r) * gs + mod_ref[0, k_shift:k_shift + 1, :]


def _group_fn(tiles_per_group, n_groups):
    return lambda i: jnp.minimum(i // tiles_per_group, n_groups - 1)


def _two_source_specs(block, split_tiles):
    return [pl.BlockSpec(block, lambda i, j: (jnp.minimum(i, split_tiles - 1), 0)),
            pl.BlockSpec(block, lambda i, j: (jnp.maximum(i - split_tiles, 0), 0))]


def _read_rows(ref, tail_ref, split_tiles):
    if tail_ref is None:
        return ref[...]
    return jnp.where(pl.program_id(0) < split_tiles, ref[...], tail_ref[...])


def _mod_kernel(c_ref, w_ref, b_ref, o_ref):
    c = c_ref[...]
    a = (c * _sigmoid(c)).astype(BF16)
    o_ref[0] = _dot(a, w_ref[0].astype(BF16)) + b_ref[0]


def mod_params(cond, w_mod, b_mod):
    depth, d, n = w_mod.shape
    g = cond.shape[0]
    tn = math.gcd(n, 1024)
    return pl.pallas_call(
        _mod_kernel,
        grid=(depth, n // tn),
        in_specs=[
            pl.BlockSpec((g, d), lambda l, j: (0, 0)),
            pl.BlockSpec((1, d, tn), lambda l, j: (l, 0, j)),
            pl.BlockSpec((1, 1, tn), lambda l, j: (l, 0, j)),
        ],
        out_specs=pl.BlockSpec((1, g, tn), lambda l, j: (l, 0, j)),
        out_shape=jax.ShapeDtypeStruct((depth, g, n), F32),
        compiler_params=_cparams("parallel", "parallel"),
        name="mod_params",
    )(cond, w_mod, b_mod.reshape(depth, 1, n))


def _ffn_kernel(*refs, k0, final_norm, split_tiles):
    it = iter(refs)
    x_ref = next(it)
    xt_ref = next(it) if split_tiles is not None else None
    g_ref, mod_ref, wg_ref, wu_ref, wo_ref = (next(it) for _ in range(5))
    gf_ref = next(it) if final_norm else None
    o_ref, h_ref, acc_ref = next(it), next(it), next(it)
    j = pl.program_id(1)

    @pl.when(j == 0)
    def _():
        x = _read_rows(x_ref, xt_ref, split_tiles)
        h_ref[...] = _modulated_norm(x, g_ref, mod_ref, k0, k0 + 1).astype(BF16)
        acc_ref[...] = jnp.zeros_like(acc_ref)

    h = h_ref[...]
    gate = _dot(h, wg_ref[...])
    up = _dot(h, wu_ref[...])
    a = (gate * _sigmoid(gate) * up).astype(BF16)
    acc_ref[...] += _dot(a, wo_ref[...])

    @pl.when(j == pl.num_programs(1) - 1)
    def _():
        x = _read_rows(x_ref, xt_ref, split_tiles)
        y = x + (0.5 * mod_ref[0, k0 + 2:k0 + 3, :]) * acc_ref[...]
        if final_norm:
            y = _rmsnorm_rows(y, gf_ref[...])
        o_ref[...] = y


def ffn_half_step(x, gain, mods, w_in, w_out, lead, *, k0, n_tiles, tiles_per_group, n_groups,
                  x_tail=None, final_gain=None, tm=FFN_ROW_TILE, tf=FFN_CHUNK):
    d = x.shape[1]
    f = w_out.shape[-2]
    nf = f // tf
    grp = _group_fn(tiles_per_group, n_groups)
    split_tiles = None if x_tail is None else x.shape[0] // tm
    sq = (None,) * len(lead)
    if x_tail is None:
        in_specs, args = [pl.BlockSpec((tm, d), lambda i, j: (i, 0))], [x]
    else:
        in_specs, args = _two_source_specs((tm, d), split_tiles), [x, x_tail]
    in_specs += [
        pl.BlockSpec((1, d), lambda i, j: (0, 0)),
        pl.BlockSpec((1, N_MOD, d), lambda i, j: (grp(i), 0, 0)),
        pl.BlockSpec(sq + (d, tf), lambda i, j: lead + (0, j)),
        pl.BlockSpec(sq + (d, tf), lambda i, j: lead + (0, j + nf)),
        pl.BlockSpec(sq + (tf, d), lambda i, j: lead + (j, 0)),
    ]
    args += [gain.reshape(1, d), mods, w_in, w_in, w_out]
    if final_gain is not None:
        in_specs.append(pl.BlockSpec((1, d), lambda i, j: (0, 0)))
        args.append(final_gain.reshape(1, d))
    return pl.pallas_call(
        functools.partial(_ffn_kernel, k0=k0, final_norm=final_gain is not None, split_tiles=split_tiles),
        grid=(n_tiles, nf),
        in_specs=in_specs,
        out_specs=pl.BlockSpec((tm, d), lambda i, j: (i, 0)),
        out_shape=jax.ShapeDtypeStruct((n_tiles * tm, d), F32),
        scratch_shapes=[pltpu.VMEM((tm, d), BF16), pltpu.VMEM((tm, d), F32)],
        compiler_params=_cparams("parallel", "arbitrary"),
        name="ffn_half_step",
    )(*args)


def _norm_linear_kernel(*refs, mod_idx, rope, layout, n_chunks):
    it = iter(refs)
    x_ref, g_ref = next(it), next(it)
    mod_ref = next(it) if mod_idx is not None else None
    w_ref = next(it)
    if rope is None:
        cs_ref = next(it)
    else:
        tab_refs = [next(it) for _ in range(len(rope) + 1)]
    o_ref, h_ref = next(it), next(it)

    @pl.when(pl.program_id(1) == 0)
    def _():
        if mod_idx is None:
            y = _rmsnorm_rows(x_ref[...], g_ref[...])
        else:
            y = _modulated_norm(x_ref[...], g_ref, mod_ref, *mod_idx)
        h_ref[...] = y.astype(BF16)

    acc = _dot(h_ref[...], w_ref[...])
    if rope is None:
        acc = acc * cs_ref[...]
    for c in range(n_chunks):
        a = acc[:, c * LANE:(c + 1) * LANE]
        if rope is not None:
            out = a * tab_refs[0][0]
            for shift, tab in zip(rope, tab_refs[1:]):
                out = out + pltpu.roll(a, shift, 1) * tab[0]
            a = out
        if layout == "heads":
            o_ref[c] = a.astype(o_ref.dtype)
        elif layout == "heads_t":
            o_ref[c] = a.T.astype(o_ref.dtype)
        else:
            o_ref[:, c * LANE:(c + 1) * LANE] = a.astype(o_ref.dtype)


def norm_linear(x, x_col_block, k, gain, w, *, n_tiles, tiles_per_group, n_groups,
                mods=None, mod_idx=None, col_scale=None, rope=None, tables=None,
                set_thresholds=(), pos_tiles=None, layout="heads", out_dtype=BF16,
                tm=PROJ_ROW_TILE, tn=512):
    n = w.shape[1]
    nj = n // tn
    n_chunks = tn // LANE
    grp = _group_fn(tiles_per_group, n_groups)
    in_specs = [
        pl.BlockSpec((tm, k), lambda i, j: (i, x_col_block)),
        pl.BlockSpec((1, k), lambda i, j: (0, 0)),
    ]
    args = [x, gain.reshape(1, k)]
    if mod_idx is not None:
        in_specs.append(pl.BlockSpec((1, N_MOD, k), lambda i, j: (grp(i), 0, 0)))
        args.append(mods)
    in_specs.append(pl.BlockSpec((k, tn), lambda i, j: (0, j)))
    args.append(w)
    if rope is None:
        if col_scale is None:
            col_scale = jnp.ones((n,), F32)
        in_specs.append(pl.BlockSpec((1, tn), lambda i, j: (0, j)))
        args.append(col_scale.reshape(1, n))
    else:
        n_lat_tiles = tiles_per_group * (n_groups - 1)

        def tab_map(i, j):
            s = sum((j >= t).astype(jnp.int32) for t in set_thresholds) if set_thresholds else 0
            p = jnp.where(i < n_lat_tiles, i % pos_tiles, pos_tiles)
            return (s, p, 0)

        for t in tables:
            in_specs.append(pl.BlockSpec((1, tm, LANE), tab_map))
            args.append(t)
    rows = n_tiles * tm
    if layout == "heads":
        out_spec = pl.BlockSpec((n_chunks, tm, LANE), lambda i, j: (j, i, 0))
        out_shape = jax.ShapeDtypeStruct((n // LANE, rows, LANE), out_dtype)
    elif layout == "heads_t":
        out_spec = pl.BlockSpec((n_chunks, LANE, tm), lambda i, j: (j, 0, i))
        out_shape = jax.ShapeDtypeStruct((n // LANE, LANE, rows), out_dtype)
    else:
        out_spec = pl.BlockSpec((tm, tn), lambda i, j: (i, j))
        out_shape = jax.ShapeDtypeStruct((rows, n), out_dtype)
    return pl.pallas_call(
        functools.partial(_norm_linear_kernel, mod_idx=mod_idx, rope=rope,
                          layout=layout, n_chunks=n_chunks),
        grid=(n_tiles, nj),
        in_specs=in_specs,
        out_specs=out_spec,
        out_shape=out_shape,
        scratch_shapes=[pltpu.VMEM((tm, k), BF16)],
        compiler_params=_cparams("parallel", "arbitrary"),
        name="norm_linear",
    )(*args)


def _linear_residual_kernel(*refs, k_gate, split_tiles):
    it = iter(refs)
    a_ref = next(it)
    at_ref = next(it) if split_tiles is not None else None
    w_ref, x_ref, mod_ref, o_ref = next(it), next(it), next(it), next(it)
    acc = _dot(_read_rows(a_ref, at_ref, split_tiles), w_ref[...])
    o_ref[...] = x_ref[...] + mod_ref[0, k_gate:k_gate + 1, :] * acc


def linear_residual(a, w, x, mods, *, k_gate, n_tiles, tiles_per_group, n_groups, a_tail=None,
                    tm=FFN_ROW_TILE):
    kdim, n = w.shape
    grp = _group_fn(tiles_per_group, n_groups)
    split_tiles = None if a_tail is None else a.shape[0] // tm
    if a_tail is None:
        in_specs, args = [pl.BlockSpec((tm, kdim), lambda i, j: (i, 0))], [a]
    else:
        in_specs, args = _two_source_specs((tm, kdim), split_tiles), [a, a_tail]
    in_specs += [
        pl.BlockSpec((kdim, n), lambda i, j: (0, 0)),
        pl.BlockSpec((tm, n), lambda i, j: (i, 0)),
        pl.BlockSpec((1, N_MOD, n), lambda i, j: (grp(i), 0, 0)),
    ]
    args += [w, x, mods]
    return pl.pallas_call(
        functools.partial(_linear_residual_kernel, k_gate=k_gate, split_tiles=split_tiles),
        grid=(n_tiles, 1),
        in_specs=in_specs,
        out_specs=pl.BlockSpec((tm, n), lambda i, j: (i, 0)),
        out_shape=jax.ShapeDtypeStruct((n_tiles * tm, n), F32),
        compiler_params=_cparams("parallel", "arbitrary"),
        name="linear_residual",
    )(*args)


NA_QROWS = 4
NA_QTOK = NA_QROWS * GRID_W
NA_BAND_BLOCKS = 3
NA_PAIRS = 2 * NA_KH


def _na_bias_kernel(rpb_ref, o_ref):
    h = pl.program_id(0)
    cq = lax.broadcasted_iota(jnp.int32, (GRID_W, 2 * GRID_W), 0)
    lane = lax.broadcasted_iota(jnp.int32, (GRID_W, 2 * GRID_W), 1)
    second = lane >= GRID_W
    ck = jnp.where(second, lane - GRID_W, lane)
    n_dc = 2 * NA_KW - 1
    code = ck - cq + (NA_KW - 1) + jnp.where(second, n_dc, 0)
    start = jnp.clip(cq - NA_KW // 2, 0, GRID_W - NA_KW)
    in_win = (ck >= start) & (ck < start + NA_KW)
    for p in range(NA_PAIRS):
        tile = jnp.zeros((GRID_W, 2 * GRID_W), F32)
        for half in range(2):
            dr = p - NA_KH + half
            if not (-(NA_KH - 1) <= dr <= NA_KH - 1):
                continue
            for dc in range(n_dc):
                val = rpb_ref[h, dr + NA_KH - 1, dc]
                tile = jnp.where(code == half * n_dc + dc, val, tile)
        o_ref[0, p] = jnp.where(in_win, tile * LOG2E, NEG_INF)


def na_bias_table(rpb):
    heads = rpb.shape[0]
    return pl.pallas_call(
        _na_bias_kernel,
        grid=(heads,),
        in_specs=[pl.BlockSpec(memory_space=pltpu.SMEM)],
        out_specs=pl.BlockSpec((1, NA_PAIRS, GRID_W, 2 * GRID_W), lambda h: (h, 0, 0, 0)),
        out_shape=jax.ShapeDtypeStruct((heads, NA_PAIRS, GRID_W, 2 * GRID_W), F32),
        compiler_params=_cparams("parallel"),
        name="na_bias_table",
    )(rpb)


def _na_kernel(q_ref, k0_ref, k1_ref, k2_ref, v0_ref, v1_ref, v2_ref, kz_ref, vz_ref,
               t2_ref, o_ref, vm_ref, ob_ref, *, rows):
    qb = pl.program_id(1)
    r0 = NA_QROWS * qb
    u0 = jnp.clip(r0 - NA_KH // 2, 0, rows - NA_BAND_BLOCKS * NA_QROWS)
    k_refs = (k0_ref, k1_ref, k2_ref, kz_ref)
    v_refs = (v0_ref, v1_ref, v2_ref, vz_ref)

    q_row = r0 + (lax.broadcasted_iota(jnp.int32, (NA_QTOK, NA_QTOK), 0) >> 6)
    k_off = lax.broadcasted_iota(jnp.int32, (NA_QTOK, NA_QTOK), 1) >> 6
    rs = jnp.clip(q_row - NA_KH // 2, 0, rows - NA_KH)
    for i in range(NA_BAND_BLOCKS):
        rk = u0 + NA_QROWS * i + k_off
        vm_ref[i] = jnp.where((rk >= rs) & (rk < rs + NA_KH), 0.0, NEG_INF)

    def scores(h):
        return _dot_t(q_ref[h], jnp.concatenate([r[h] for r in k_refs], axis=0))

    def finish_head(h, s_all):
        s = []
        for i in range(NA_BAND_BLOCKS):
            bias_rows = []
            for qi in range(NA_QROWS):
                pair = []
                for c in range(2):
                    dr0 = u0 + NA_QROWS * i + 2 * c - (r0 + qi)
                    pair.append(t2_ref[h, jnp.clip(dr0 + NA_KH, 0, NA_PAIRS - 1)])
                bias_rows.append(jnp.concatenate(pair, axis=1))
            s.append(s_all[:, i * NA_QTOK:(i + 1) * NA_QTOK] + jnp.concatenate(bias_rows, axis=0) + vm_ref[i])
        s.append(s_all[:, NA_BAND_BLOCKS * NA_QTOK:])
        m = jnp.maximum(jnp.maximum(s[0], s[1]), jnp.maximum(s[2], s[3])).max(axis=-1, keepdims=True)
        p = [jnp.exp2(si - m) for si in s]
        l = ((p[0] + p[1]) + (p[2] + p[3])).sum(axis=-1, keepdims=True)
        o = _dot(jnp.concatenate(p, axis=1).astype(BF16), jnp.concatenate([r[h] for r in v_refs], axis=0))
        ob_ref[h] = (o * (1.0 / l)).astype(BF16)

    s_next = scores(0)
    for h in range(NA_HEADS):
        s_cur = s_next
        if h + 1 < NA_HEADS:
            s_next = scores(h + 1)
        finish_head(h, s_cur)
    for h in range(NA_HEADS):
        o_ref[:, h * HEAD_DIM:(h + 1) * HEAD_DIM] = ob_ref[h]


def na_attention(qkv, t2, *, batch, seq, ctx_len):
    rows = seq // GRID_W
    nqb = seq // NA_QTOK
    z0 = batch * seq // ctx_len
    assert ctx_len == NA_QTOK and rows >= NA_BAND_BLOCKS * NA_QROWS
    blk = (NA_HEADS, NA_QTOK, HEAD_DIM)

    def band(i):
        return lambda b, q: (0, b * nqb + jnp.clip(q - 1, 0, nqb - NA_BAND_BLOCKS) + i, 0)

    def shifted(fn, part):
        return lambda b, q: (part,) + fn(b, q)[1:]

    in_specs = [pl.BlockSpec(blk, lambda b, q: (0, b * nqb + q, 0))]
    in_specs += [pl.BlockSpec(blk, shifted(band(i), 1)) for i in range(NA_BAND_BLOCKS)]
    in_specs += [pl.BlockSpec(blk, shifted(band(i), 2)) for i in range(NA_BAND_BLOCKS)]
    in_specs += [pl.BlockSpec(blk, lambda b, q: (1, z0 + b, 0)),
                 pl.BlockSpec(blk, lambda b, q: (2, z0 + b, 0)),
                 pl.BlockSpec(t2.shape, lambda b, q: (0, 0, 0, 0))]
    return pl.pallas_call(
        functools.partial(_na_kernel, rows=rows),
        grid=(batch, nqb),
        in_specs=in_specs,
        out_specs=pl.BlockSpec((NA_QTOK, NA_HEADS * HEAD_DIM), lambda b, q: (b * nqb + q, 0)),
        out_shape=jax.ShapeDtypeStruct((batch * seq, NA_HEADS * HEAD_DIM), BF16),
        scratch_shapes=[pltpu.VMEM((NA_BAND_BLOCKS, NA_QTOK, NA_QTOK), F32),
                        pltpu.VMEM(blk, BF16)],
        compiler_params=_cparams("parallel", "arbitrary"),
        name="na_attention",
    )(*([qkv] * 9), t2)


MLA_TQ = 1024
MLA_TK = 1024


def _mla_kernel(qn_ref, qp_ref, kn_ref, kp_ref, vt_ref, knz_ref, kpz_ref, vtz_ref, o_ref, *, seq):
    q = jnp.concatenate([qn_ref[0], qp_ref[0]], axis=1)
    tq = q.shape[0]

    n_blocks = seq // MLA_TK

    def scores(kb):
        if kb == n_blocks:
            return _dot_t(jnp.concatenate([knz_ref[0], kpz_ref[0]], axis=1), q)
        rows = slice(kb * MLA_TK, (kb + 1) * MLA_TK)
        return _dot_t(jnp.concatenate([kn_ref[0, rows, :], kp_ref[0, rows, :]], axis=1), q)

    m = jnp.full((1, tq), NEG_INF, F32)
    l = jnp.zeros((1, tq), F32)
    acc = jnp.zeros((MLA_V, tq), F32)
    s_next = scores(0)
    for kb in range(n_blocks + 1):
        s = s_next
        if kb < n_blocks:
            s_next = scores(kb + 1)
            vt = vt_ref[0, :, kb * MLA_TK:(kb + 1) * MLA_TK]
        else:
            vt = vtz_ref[0]
        m_new = jnp.maximum(m, s.max(axis=0, keepdims=True))
        alpha = jnp.exp2(m - m_new)
        p = jnp.exp2(s - m_new)
        l = alpha * l + p.sum(axis=0, keepdims=True)
        acc = alpha * acc + _dot(vt, p.astype(BF16))
        m = m_new
    o_ref[...] = (acc * (1.0 / l)).T.astype(BF16)


def mla_attention(qq, kn, kp, vt, *, batch, seq, ctx_len):
    h_ = MLA_HEADS
    nqb = seq // MLA_TQ
    z0 = batch * seq // ctx_len
    qblk = (1, MLA_TQ, LANE)
    kblk = (1, seq, LANE)
    zblk = (1, ctx_len, LANE)
    in_specs = [
        pl.BlockSpec(qblk, lambda b, h, q: (h, b * nqb + q, 0)),
        pl.BlockSpec(qblk, lambda b, h, q: (h_ + h, b * nqb + q, 0)),
        pl.BlockSpec(kblk, lambda b, h, q: (h, b, 0)),
        pl.BlockSpec(kblk, lambda b, h, q: (0, b, 0)),
        pl.BlockSpec((1, MLA_V, seq), lambda b, h, q: (h, 0, b)),
        pl.BlockSpec(zblk, lambda b, h, q: (h, z0 + b, 0)),
        pl.BlockSpec(zblk, lambda b, h, q: (0, z0 + b, 0)),
        pl.BlockSpec((1, MLA_V, ctx_len), lambda b, h, q: (h, 0, z0 + b)),
    ]
    return pl.pallas_call(
        functools.partial(_mla_kernel, seq=seq),
        grid=(batch, h_, nqb),
        in_specs=in_specs,
        out_specs=pl.BlockSpec((MLA_TQ, MLA_V), lambda b, h, q: (b * nqb + q, h)),
        out_shape=jax.ShapeDtypeStruct((batch * seq, h_ * MLA_V), BF16),
        compiler_params=_cparams("parallel", "parallel", "arbitrary"),
        name="mla_attention",
    )(qq, qq, kn, kp, vt, kn, kp, vt)


SWA_TQ = SWA_WINDOW


def _swa_kernel(sink_ref, q_ref, k0_ref, k1_ref, k2_ref, v0_ref, v1_ref, v2_ref,
                kz_ref, vz_ref, o_ref, *, n_blocks):
    qb = pl.program_id(1)
    g = SWA_HEADS // SWA_KV_HEADS
    m_rows = g * SWA_TQ
    row = lax.broadcasted_iota(jnp.int32, (m_rows, SWA_TQ), 0) & (SWA_TQ - 1)
    lane = lax.broadcasted_iota(jnp.int32, (m_rows, SWA_TQ), 1)
    mask_prev = (lane >= row) & (qb >= 1)
    mask_next = (lane <= row) & (qb <= n_blocks - 2)
    grp_id = lax.broadcasted_iota(jnp.int32, (m_rows, 1), 0) >> 7
    k_refs = (k0_ref, k1_ref, k2_ref, kz_ref)
    v_refs = (v0_ref, v1_ref, v2_ref, vz_ref)

    def scores(kvh):
        q4 = q_ref[kvh * g:(kvh + 1) * g].reshape(m_rows, HEAD_DIM)
        return _dot_t(q4, jnp.concatenate([r[kvh] for r in k_refs], axis=0))

    s_next = scores(0)
    for kvh in range(SWA_KV_HEADS):
        s_all = s_next
        if kvh + 1 < SWA_KV_HEADS:
            s_next = scores(kvh + 1)
        sink = jnp.zeros((m_rows, 1), F32)
        for gi in range(g):
            sink = jnp.where(grp_id == gi, sink_ref[kvh * g + gi] * LOG2E, sink)
        s = [jnp.where(mask_prev, s_all[:, :SWA_TQ], NEG_INF),
             s_all[:, SWA_TQ:2 * SWA_TQ],
             jnp.where(mask_next, s_all[:, 2 * SWA_TQ:3 * SWA_TQ], NEG_INF)]
        s += [s_all[:, c:c + SWA_TQ] for c in range(3 * SWA_TQ, s_all.shape[1], SWA_TQ)]
        mx = s[0]
        for si in s[1:]:
            mx = jnp.maximum(mx, si)
        m = jnp.maximum(mx.max(axis=-1, keepdims=True), sink)
        p = [jnp.exp2(si - m) for si in s]
        ps = p[0]
        for pi in p[1:]:
            ps = ps + pi
        l = ps.sum(axis=-1, keepdims=True) + jnp.exp2(sink - m)
        o = _dot(jnp.concatenate(p, axis=1).astype(BF16), jnp.concatenate([r[kvh] for r in v_refs], axis=0))
        o = (o * (1.0 / l)).astype(BF16)
        for gi in range(g):
            hq = kvh * g + gi
            o_ref[:, hq * HEAD_DIM:(hq + 1) * HEAD_DIM] = o[gi * SWA_TQ:(gi + 1) * SWA_TQ]


def swa_attention(qkv, sink, *, batch, seq, ctx_len):
    nb = seq // SWA_TQ
    kvh = SWA_KV_HEADS
    qpart = SWA_HEADS // kvh
    z0 = batch * seq // ctx_len
    kblk = (kvh, SWA_TQ, HEAD_DIM)
    zblk = (kvh, ctx_len, HEAD_DIM)
    assert ctx_len % SWA_TQ == 0

    def kmap(part, d):
        return lambda b, q: (part, b * nb + jnp.clip(q + d, 0, nb - 1), 0)

    in_specs = [pl.BlockSpec(memory_space=pltpu.SMEM),
                pl.BlockSpec((SWA_HEADS, SWA_TQ, HEAD_DIM), lambda b, q: (0, b * nb + q, 0))]
    in_specs += [pl.BlockSpec(kblk, kmap(qpart, d)) for d in (-1, 0, 1)]
    in_specs += [pl.BlockSpec(kblk, kmap(qpart + 1, d)) for d in (-1, 0, 1)]
    in_specs += [pl.BlockSpec(zblk, lambda b, q: (qpart, z0 + b, 0)),
                 pl.BlockSpec(zblk, lambda b, q: (qpart + 1, z0 + b, 0))]
    return pl.pallas_call(
        functools.partial(_swa_kernel, n_blocks=nb),
        grid=(batch, nb),
        in_specs=in_specs,
        out_specs=pl.BlockSpec((SWA_TQ, SWA_HEADS * HEAD_DIM), lambda b, q: (b * nb + q, 0)),
        out_shape=jax.ShapeDtypeStruct((batch * seq, SWA_HEADS * HEAD_DIM), BF16),
        compiler_params=_cparams("parallel", "arbitrary"),
        name="swa_attention",
    )(sink, *([qkv] * 9))


def _ctx_kernel(*refs, two_part, use_sink, v_transposed):
    it = iter(refs)
    sink_ref = next(it) if use_sink else None
    if two_part:
        q = jnp.concatenate([next(it)[0], next(it)[0]], axis=1)
        k = jnp.concatenate([next(it)[0], next(it)[0]], axis=1)
    else:
        q = next(it)[0]
        k = next(it)[0]
    v_ref, o_ref = next(it), next(it)
    s = _dot_t(q, k)
    m = s.max(axis=-1, keepdims=True)
    if use_sink:
        sink = sink_ref[pl.program_id(1)] * LOG2E
        m = jnp.maximum(m, sink)
    p = jnp.exp2(s - m)
    l = p.sum(axis=-1, keepdims=True)
    if use_sink:
        l = l + jnp.exp2(sink - m)
    pv = _dot_t(p.astype(BF16), v_ref[0]) if v_transposed else _dot(p.astype(BF16), v_ref[0])
    o_ref[...] = (pv * (1.0 / l)).astype(BF16)


def ctx_attention(q_parts, k_parts, v_part, *, batch, ctx_len, z0, n_heads, sink=None,
                  v_transposed=False):
    blk = (1, ctx_len, LANE)
    in_specs, args = [], []
    if sink is not None:
        in_specs.append(pl.BlockSpec(memory_space=pltpu.SMEM))
        args.append(sink)
    for arr, head_fn in (*q_parts, *k_parts):
        in_specs.append(pl.BlockSpec(blk, functools.partial(
            lambda b, h, head_fn: (head_fn(h), z0 + b, 0), head_fn=head_fn)))
        args.append(arr)
    v_arr, v_head = v_part
    if v_transposed:
        in_specs.append(pl.BlockSpec((1, LANE, ctx_len), lambda b, h: (v_head(h), 0, z0 + b)))
    else:
        in_specs.append(pl.BlockSpec(blk, lambda b, h: (v_head(h), z0 + b, 0)))
    args.append(v_arr)
    return pl.pallas_call(
        functools.partial(_ctx_kernel, two_part=len(q_parts) == 2, use_sink=sink is not None,
                          v_transposed=v_transposed),
        grid=(batch, n_heads),
        in_specs=in_specs,
        out_specs=pl.BlockSpec((ctx_len, LANE), lambda b, h: (b, h)),
        out_shape=jax.ShapeDtypeStruct((batch * ctx_len, n_heads * LANE), BF16),
        compiler_params=_cparams("parallel", "parallel"),
        name="ctx_attention",
    )(*args)


def _rope_angles(n_tokens, rot_dim):
    t = jnp.arange(n_tokens)
    row = (t // GRID_W).astype(F32)
    col = (t % GRID_W).astype(F32)
    n_freq = rot_dim // 4
    inv_freq = ROPE_BASE ** (-jnp.arange(n_freq, dtype=F32) / n_freq)
    ang = jnp.concatenate([row[:, None] * inv_freq, col[:, None] * inv_freq], axis=-1)
    return jnp.cos(ang), jnp.sin(ang)


def _with_identity_tile(tab, fill, tm):
    return jnp.concatenate([tab, jnp.broadcast_to(fill, (tm, LANE))], axis=0)


def swa_rope_tables(seq, scale, tm):
    cos, sin = _rope_angles(seq, HEAD_DIM)
    c = jnp.concatenate([cos, cos], axis=-1)
    s = jnp.concatenate([-sin, sin], axis=-1)
    one = jnp.ones((LANE,), F32)
    zero = jnp.zeros((LANE,), F32)
    ident_c = _with_identity_tile(jnp.ones_like(c), one, tm)
    ident_s = _with_identity_tile(jnp.zeros_like(s), zero, tm)
    c_sets = jnp.stack([_with_identity_tile(c * scale, one * scale, tm), _with_identity_tile(c, one, tm), ident_c])
    s_sets = jnp.stack([_with_identity_tile(s * scale, zero, tm), _with_identity_tile(s, zero, tm), ident_s])
    return c_sets, s_sets


def mla_rope_tables(seq, scale, tm):
    cos, sin = _rope_angles(seq, MLA_ROPE)
    half = MLA_ROPE // 2
    z = jnp.zeros((seq, half), F32)
    c = jnp.concatenate([cos, cos, z, z], axis=-1)
    s1 = jnp.concatenate([-sin, z, z, z], axis=-1)
    s2 = jnp.concatenate([z, sin, z, z], axis=-1)
    one = jnp.ones((LANE,), F32)
    zero = jnp.zeros((LANE,), F32)
    wt = _with_identity_tile
    c_sets = jnp.stack([wt(jnp.ones_like(c) * scale, one * scale, tm), wt(c * scale, one * scale, tm), wt(c, one, tm)])
    s1_sets = jnp.stack([wt(jnp.zeros_like(c), zero, tm), wt(s1 * scale, zero, tm), wt(s1, zero, tm)])
    s2_sets = jnp.stack([wt(jnp.zeros_like(c), zero, tm), wt(s2 * scale, zero, tm), wt(s2, zero, tm)])
    return c_sets, s1_sets, s2_sets


def kernel(x, c, ctx, c_ctx, w_mod, b_mod, norm_g, ffn_w_in, ffn_w_out, na_w_qkv, na_rpb, na_w_o,
           mla_w_down, mla_q_norm_g, mla_w_q_up, mla_kv_norm_g, mla_w_kv_up, mla_w_o,
           swa_w_qkv, swa_sink, swa_w_o, final_norm_g):
    batch, seq, d = x.shape
    ctx_len = ctx.shape[1]
    depth = w_mod.shape[0]
    tf_, tp = FFN_ROW_TILE, PROJ_ROW_TILE
    nx, nz = batch * seq, batch * ctx_len
    n_groups = batch + 1
    z0 = nx // ctx_len
    assert seq % tp == 0 and nz % tp == 0 and nx % ctx_len == 0 and tp % tf_ == 0
    ffn_rows = dict(tiles_per_group=seq // tf_, n_groups=n_groups)
    proj_rows = dict(tiles_per_group=seq // tp, n_groups=n_groups, n_tiles=(nx + nz) // tp)
    pos_tiles = seq // tp

    n_cond = 8
    cond = jnp.concatenate([c, c_ctx[None, :], jnp.zeros((n_cond - batch - 1, d), F32)], axis=0)
    mods_all = mod_params(cond, w_mod, b_mod).reshape(depth, n_cond, N_MOD, d)

    w_in_all = ffn_w_in.astype(BF16)
    w_out_all = ffn_w_out.astype(BF16)
    qk_scale = HEAD_DIM ** -0.5 * LOG2E
    mla_scale = (MLA_NOPE + MLA_ROPE) ** -0.5 * LOG2E

    stream = None
    for li in range(depth):
        last = li == depth - 1
        j = li // N_MIXERS
        kind = li % N_MIXERS
        mods = mods_all[li]

        if li == 0:
            stream = ffn_half_step(x.reshape(nx, d), norm_g[li, 0], mods, w_in_all, w_out_all, (li, 0), k0=0,
                                   n_tiles=(nx + nz) // tf_, x_tail=ctx.reshape(nz, d), **ffn_rows)
        else:
            stream = ffn_half_step(stream, norm_g[li, 0], mods, w_in_all, w_out_all, (li, 0), k0=0,
                                   n_tiles=(nx + nz) // tf_, **ffn_rows)

        proj = dict(mods=mods, mod_idx=(3, 4), **proj_rows)
        if kind == 0:
            col_scale = jnp.concatenate([jnp.full((NA_HEADS * HEAD_DIM,), qk_scale, F32),
                                         jnp.ones((2 * NA_HEADS * HEAD_DIM,), F32)])
            qkv = norm_linear(stream, 0, d, norm_g[li, 1], na_w_qkv[j].astype(BF16),
                              col_scale=col_scale, tn=1024, **proj)
            t2 = na_bias_table(na_rpb[j])
            ox = na_attention(qkv, t2, batch=batch, seq=seq, ctx_len=ctx_len)
            if not last:
                hd = lambda part: (qkv, lambda h: part * NA_HEADS + h)
                oz = ctx_attention([hd(0)], [hd(1)], hd(2), batch=batch, ctx_len=ctx_len,
                                   z0=z0, n_heads=NA_HEADS)
            w_o = na_w_o[j]
        elif kind == 1:
            w_down = mla_w_down[j]
            lora = MLA_Q_LORA + MLA_KV_LORA
            cqkv = norm_linear(stream, 0, d, norm_g[li, 1], w_down[:, :lora].astype(BF16),
                               layout="rows", out_dtype=F32, tn=lora, **proj)
            c_t, s1_t, s2_t = mla_rope_tables(seq, mla_scale, tp)
            w_kpe = jnp.pad(w_down[:, lora:], ((0, 0), (0, LANE - MLA_ROPE))).astype(BF16)
            kp = norm_linear(stream, 0, d, norm_g[li, 1], w_kpe, rope=(96, 32),
                             tables=(c_t[2:], s1_t[2:], s2_t[2:]), pos_tiles=pos_tiles,
                             tn=LANE, **proj)
            wq = mla_w_q_up[j].reshape(MLA_Q_LORA, MLA_HEADS, MLA_NOPE + MLA_ROPE)
            wq_n = wq[:, :, :MLA_NOPE].reshape(MLA_Q_LORA, MLA_HEADS * MLA_NOPE)
            wq_p = jnp.pad(wq[:, :, MLA_NOPE:], ((0, 0), (0, 0), (0, LANE - MLA_ROPE)))
            wq_all = jnp.concatenate([wq_n, wq_p.reshape(MLA_Q_LORA, MLA_HEADS * LANE)], axis=1).astype(BF16)
            qq = norm_linear(cqkv, 0, MLA_Q_LORA, mla_q_norm_g[j], wq_all, rope=(96, 32),
                             tables=(c_t[:2], s1_t[:2], s2_t[:2]),
                             set_thresholds=(MLA_HEADS * MLA_NOPE // 512,),
                             pos_tiles=pos_tiles, tn=512, **proj_rows)
            wkv = mla_w_kv_up[j].reshape(MLA_KV_LORA, MLA_HEADS, MLA_NOPE + MLA_V)
            w_kn = wkv[:, :, :MLA_NOPE].reshape(MLA_KV_LORA, -1).astype(BF16)
            w_v = wkv[:, :, MLA_NOPE:].reshape(MLA_KV_LORA, -1).astype(BF16)
            kn = norm_linear(cqkv, 1, MLA_KV_LORA, mla_kv_norm_g[j], w_kn, tn=1024, **proj_rows)
            vt = norm_linear(cqkv, 1, MLA_KV_LORA, mla_kv_norm_g[j], w_v, layout="heads_t", tn=1024, **proj_rows)
            ox = mla_attention(qq, kn, kp, vt, batch=batch, seq=seq, ctx_len=ctx_len)
            if not last:
                oz = ctx_attention([(qq, lambda h: h), (qq, lambda h: MLA_HEADS + h)],
                                   [(kn, lambda h: h), (kp, lambda h: 0)], (vt, lambda h: h),
                                   batch=batch, ctx_len=ctx_len, z0=z0, n_heads=MLA_HEADS, v_transposed=True)
            w_o = mla_w_o[j]
        else:
            c_t, s_t = swa_rope_tables(seq, qk_scale, tp)
            n_q_blocks = SWA_HEADS * HEAD_DIM // 512
            qkv = norm_linear(stream, 0, d, norm_g[li, 1], swa_w_qkv[j].astype(BF16), rope=(64,),
                              tables=(c_t, s_t), set_thresholds=(n_q_blocks, n_q_blocks + 1),
                              pos_tiles=pos_tiles, tn=512, **proj)
            ox = swa_attention(qkv, swa_sink[j], batch=batch, seq=seq, ctx_len=ctx_len)
            if not last:
                g = SWA_HEADS // SWA_KV_HEADS
                oz = ctx_attention([(qkv, lambda h: h)], [(qkv, lambda h: SWA_HEADS + h // g)],
                                   (qkv, lambda h: SWA_HEADS + SWA_KV_HEADS + h // g),
                                   batch=batch, ctx_len=ctx_len, z0=z0, n_heads=SWA_HEADS,
                                   sink=swa_sink[j])
            w_o = swa_w_o[j]

        n_tiles = nx // tf_ if last else (nx + nz) // tf_
        stream = linear_residual(ox, w_o.astype(BF16), stream, mods, k_gate=5, n_tiles=n_tiles,
                                 a_tail=None if last else oz, **ffn_rows)
        stream = ffn_half_step(stream, norm_g[li, 2], mods, w_in_all, w_out_all, (li, 1), k0=6,
                               n_tiles=n_tiles, final_gain=final_norm_g if last else None, **ffn_rows)

    return stream.reshape(batch, seq, d)
```

```python
import functools
import math

import jax
import jax.numpy as jnp
from jax import lax
from jax.experimental import pallas as pl
from jax.experimental.pallas import tpu as pltpu

DEPTH = 4
GRID_W = 64
N_MIXERS = 3
N_MOD = 9
RMS_EPS = 1e-6
ROPE_BASE = 10000.0
NEG_INF = -1e30
LOG2E = math.log2(math.e)

NA_HEADS = 16
NA_KH = 8
NA_KW = 16
HEAD_DIM = 128

MLA_HEADS = 16
MLA_Q_LORA = 512
MLA_KV_LORA = 512
MLA_NOPE = 128
MLA_ROPE = 64
MLA_V = 128

SWA_HEADS = 16
SWA_KV_HEADS = 4
SWA_WINDOW = 128

LANE = 128
FFN_ROW_TILE = 512
OUT_ROW_TILE = 512
PROJ_ROW_TILE = 1024
FFN_CHUNK = 512
VMEM_LIMIT = 56 * 1024 * 1024

F32 = jnp.float32
BF16 = jnp.bfloat16


def _cparams(*sem):
    return pltpu.CompilerParams(dimension_semantics=sem, vmem_limit_bytes=VMEM_LIMIT)


def _dot(a, b):
    return jnp.dot(a, b, preferred_element_type=F32)


def _dot_t(a, b):
    return lax.dot_general(a, b, (((1,), (1,)), ((), ())), preferred_element_type=F32)


def _sigmoid(x):
    return 1.0 / (1.0 + jnp.exp(-x))


def _rmsnorm_rows(x, g):
    ms = jnp.mean(x * x, axis=-1, keepdims=True)
    return x * lax.rsqrt(ms + RMS_EPS) * g


def _modulated_norm(x, g_ref, mod_ref, k_shift, k_scale):
    r = lax.rsqrt(jnp.mean(x * x, axis=-1, keepdims=True) + RMS_EPS)
    gs = g_ref[...] * (1.0 + mod_ref[0, k_scale:k_scale + 1, :])
    return (x * r) * gs + mod_ref[0, k_shift:k_shift + 1, :]


def _group_fn(tiles_per_group, n_groups):
    return lambda i: jnp.minimum(i // tiles_per_group, n_groups - 1)


def _two_source_specs(block, split_tiles):
    return [pl.BlockSpec(block, lambda i, j: (jnp.minimum(i, split_tiles - 1), 0)),
            pl.BlockSpec(block, lambda i, j: (jnp.maximum(i - split_tiles, 0), 0))]


def _read_rows(ref, tail_ref, split_tiles):
    if tail_ref is None:
        return ref[...]
    return jnp.where(pl.program_id(0) < split_tiles, ref[...], tail_ref[...])


def _mod_kernel(c_ref, w_ref, b_ref, o_ref):
    c = c_ref[...]
    a = (c * _sigmoid(c)).astype(BF16)
    o_ref[0] = _dot(a, w_ref[0].astype(BF16)) + b_ref[0]


def mod_params(cond, w_mod, b_mod):
    depth, d, n = w_mod.shape
    g = cond.shape[0]
    tn = math.gcd(n, 1024)
    return pl.pallas_call(
        _mod_kernel,
        grid=(depth, n // tn),
        in_specs=[
            pl.BlockSpec((g, d), lambda l, j: (0, 0)),
            pl.BlockSpec((1, d, tn), lambda l, j: (l, 0, j)),
            pl.BlockSpec((1, 1, tn), lambda l, j: (l, 0, j)),
        ],
        out_specs=pl.BlockSpec((1, g, tn), lambda l, j: (l, 0, j)),
        out_shape=jax.ShapeDtypeStruct((depth, g, n), F32),
        compiler_params=_cparams("parallel", "parallel"),
        name="mod_params",
    )(cond, w_mod, b_mod.reshape(depth, 1, n))


def _ffn_kernel(*refs, k0, final_norm, split_tiles, ahead_rows):
    it = iter(refs)
    x_ref = next(it)
    xt_ref = next(it) if split_tiles is not None else None
    if ahead_rows is not None:
        xn_ref, modn_ref = next(it), next(it)
    g_ref, mod_ref, wg_ref, wu_ref, wo_ref = (next(it) for _ in range(5))
    gf_ref = next(it) if final_norm else None
    o_ref, h_ref = next(it), next(it)
    i, j = pl.program_id(0), pl.program_id(1)
    tm = o_ref.shape[0]

    if ahead_rows is None:
        @pl.when(j == 0)
        def _():
            x = _read_rows(x_ref, xt_ref, split_tiles)
            h_ref[0] = _modulated_norm(x, g_ref, mod_ref, k0, k0 + 1).astype(BF16)
            o_ref[...] = jnp.zeros_like(o_ref)

        h = h_ref[0]
    else:
        slot = i & 1

        @pl.when((i == 0) & (j == 0))
        def _():
            h_ref[0] = _modulated_norm(x_ref[...], g_ref, mod_ref, k0, k0 + 1).astype(BF16)

        @pl.when(j == 0)
        def _():
            o_ref[...] = jnp.zeros_like(o_ref)

        h = h_ref[slot]

    gate = _dot(h, wg_ref[...])
    up = _dot(h, wu_ref[...])
    a = (gate * _sigmoid(gate) * up).astype(BF16)
    o_ref[...] += _dot(a, wo_ref[...])

    if ahead_rows is not None:
        r0 = pl.multiple_of(jnp.minimum(j * ahead_rows, tm - ahead_rows), 16)
        hn = _modulated_norm(xn_ref[pl.ds(r0, ahead_rows), :], g_ref, modn_ref, k0, k0 + 1)
        h_ref[1 - slot, pl.ds(r0, ahead_rows), :] = hn.astype(BF16)

    @pl.when(j == pl.num_programs(1) - 1)
    def _():
        x = _read_rows(x_ref, xt_ref, split_tiles)
        y = x + (0.5 * mod_ref[0, k0 + 2:k0 + 3, :]) * o_ref[...]
        if final_norm:
            y = _rmsnorm_rows(y, gf_ref[...])
        o_ref[...] = y


def ffn_half_step(x, gain, mods, w_in, w_out, lead, *, k0, n_tiles, tiles_per_group, n_groups,
                  x_tail=None, final_gain=None, tm=FFN_ROW_TILE, tf=FFN_CHUNK):
    d = x.shape[1]
    f = w_out.shape[-2]
    nf = f // tf
    grp = _group_fn(tiles_per_group, n_groups)
    split_tiles = None if x_tail is None else x.shape[0] // tm
    sq = (None,) * len(lead)
    ahead_rows = None
    if x_tail is None:
        ahead_rows = -(-(-(-tm // nf)) // 16) * 16
        nxt = lambda i: jnp.minimum(i + 1, n_tiles - 1)
        in_specs = [pl.BlockSpec((tm, d), lambda i, j: (i, 0)),
                    pl.BlockSpec((tm, d), lambda i, j: (nxt(i), 0)),
                    pl.BlockSpec((1, N_MOD, d), lambda i, j: (grp(nxt(i)), 0, 0))]
        args = [x, x, mods]
    else:
        in_specs, args = _two_source_specs((tm, d), split_tiles), [x, x_tail]
    in_specs += [
        pl.BlockSpec((1, d), lambda i, j: (0, 0)),
        pl.BlockSpec((1, N_MOD, d), lambda i, j: (grp(i), 0, 0)),
        pl.BlockSpec(sq + (d, tf), lambda i, j: lead + (0, j)),
        pl.BlockSpec(sq + (d, tf), lambda i, j: lead + (0, j + nf)),
        pl.BlockSpec(sq + (tf, d), lambda i, j: lead + (j, 0)),
    ]
    args += [gain.reshape(1, d), mods, w_in, w_in, w_out]
    if final_gain is not None:
        in_specs.append(pl.BlockSpec((1, d), lambda i, j: (0, 0)))
        args.append(final_gain.reshape(1, d))
    return pl.pallas_call(
        functools.partial(_ffn_kernel, k0=k0, final_norm=final_gain is not None, split_tiles=split_tiles,
                          ahead_rows=ahead_rows),
        grid=(n_tiles, nf),
        in_specs=in_specs,
        out_specs=pl.BlockSpec((tm, d), lambda i, j: (i, 0)),
        out_shape=jax.ShapeDtypeStruct((n_tiles * tm, d), F32),
        scratch_shapes=[pltpu.VMEM((1 if ahead_rows is None else 2, tm, d), BF16)],
        compiler_params=_cparams("arbitrary", "arbitrary"),
        name="ffn_half_step",
    )(*args)


def _norm_linear_kernel(*refs, mod_idx, rope, layout, n_chunks):
    it = iter(refs)
    x_ref, g_ref = next(it), next(it)
    mod_ref = next(it) if mod_idx is not None else None
    w_ref = next(it)
    if rope is None:
        cs_ref = next(it)
    else:
        tab_refs = [next(it) for _ in range(len(rope) + 1)]
    o_ref, h_ref = next(it), next(it)

    @pl.when(pl.program_id(1) == 0)
    def _():
        if mod_idx is None:
            y = _rmsnorm_rows(x_ref[...], g_ref[...])
        else:
            y = _modulated_norm(x_ref[...], g_ref, mod_ref, *mod_idx)
        h_ref[...] = y.astype(BF16)

    acc = _dot(h_ref[...], w_ref[...])
    if rope is None:
        acc = acc * cs_ref[...]
    for c in range(n_chunks):
        a = acc[:, c * LANE:(c + 1) * LANE]
        if rope is not None:
            out = a * tab_refs[0][0]
            for shift, tab in zip(rope, tab_refs[1:]):
                out = out + pltpu.roll(a, shift, 1) * tab[0]
            a = out
        if layout == "heads":
            o_ref[c] = a.astype(o_ref.dtype)
        elif layout == "heads_t":
            o_ref[c] = a.T.astype(o_ref.dtype)
        else:
            o_ref[:, c * LANE:(c + 1) * LANE] = a.astype(o_ref.dtype)


def norm_linear(x, x_col_block, k, gain, w, *, n_tiles, tiles_per_group, n_groups,
                mods=None, mod_idx=None, col_scale=None, rope=None, tables=None,
                set_thresholds=(), pos_tiles=None, layout="heads", out_dtype=BF16,
                tm=PROJ_ROW_TILE, tn=512):
    n = w.shape[1]
    nj = n // tn
    n_chunks = tn // LANE
    grp = _group_fn(tiles_per_group, n_groups)
    in_specs = [
        pl.BlockSpec((tm, k), lambda i, j: (i, x_col_block)),
        pl.BlockSpec((1, k), lambda i, j: (0, 0)),
    ]
    args = [x, gain.reshape(1, k)]
    if mod_idx is not None:
        in_specs.append(pl.BlockSpec((1, N_MOD, k), lambda i, j: (grp(i), 0, 0)))
        args.append(mods)
    in_specs.append(pl.BlockSpec((k, tn), lambda i, j: (0, j)))
    args.append(w)
    if rope is None:
        if col_scale is None:
            col_scale = jnp.ones((n,), F32)
        in_specs.append(pl.BlockSpec((1, tn), lambda i, j: (0, j)))
        args.append(col_scale.reshape(1, n))
    else:
        n_lat_tiles = tiles_per_group * (n_groups - 1)

        def tab_map(i, j):
            s = sum((j >= t).astype(jnp.int32) for t in set_thresholds) if set_thresholds else 0
            p = jnp.where(i < n_lat_tiles, i % pos_tiles, pos_tiles)
            return (s, p, 0)

        for t in tables:
            in_specs.append(pl.BlockSpec((1, tm, LANE), tab_map))
            args.append(t)
    rows = n_tiles * tm
    if layout == "heads":
        out_spec = pl.BlockSpec((n_chunks, tm, LANE), lambda i, j: (j, i, 0))
        out_shape = jax.ShapeDtypeStruct((n // LANE, rows, LANE), out_dtype)
    elif layout == "heads_t":
        out_spec = pl.BlockSpec((n_chunks, LANE, tm), lambda i, j: (j, 0, i))
        out_shape = jax.ShapeDtypeStruct((n // LANE, LANE, rows), out_dtype)
    else:
        out_spec = pl.BlockSpec((tm, tn), lambda i, j: (i, j))
        out_shape = jax.ShapeDtypeStruct((rows, n), out_dtype)
    return pl.pallas_call(
        functools.partial(_norm_linear_kernel, mod_idx=mod_idx, rope=rope,
                          layout=layout, n_chunks=n_chunks),
        grid=(n_tiles, nj),
        in_specs=in_specs,
        out_specs=out_spec,
        out_shape=out_shape,
        scratch_shapes=[pltpu.VMEM((tm, k), BF16)],
        compiler_params=_cparams("parallel", "arbitrary"),
        name="norm_linear",
    )(*args)


def _linear_residual_kernel(*refs, k_gate, split_tiles):
    it = iter(refs)
    a_ref = next(it)
    at_ref = next(it) if split_tiles is not None else None
    w_ref, x_ref, mod_ref, o_ref = next(it), next(it), next(it), next(it)
    acc = _dot(_read_rows(a_ref, at_ref, split_tiles), w_ref[...])
    o_ref[...] = x_ref[...] + mod_ref[0, k_gate:k_gate + 1, :] * acc


def linear_residual(a, w, x, mods, *, k_gate, n_tiles, tiles_per_group, n_groups, a_tail=None,
                    tm=OUT_ROW_TILE):
    kdim, n = w.shape
    grp = _group_fn(tiles_per_group, n_groups)
    split_tiles = None if a_tail is None else a.shape[0] // tm
    if a_tail is None:
        in_specs, args = [pl.BlockSpec((tm, kdim), lambda i, j: (i, 0))], [a]
    else:
        in_specs, args = _two_source_specs((tm, kdim), split_tiles), [a, a_tail]
    in_specs += [
        pl.BlockSpec((kdim, n), lambda i, j: (0, 0)),
        pl.BlockSpec((tm, n), lambda i, j: (i, 0)),
        pl.BlockSpec((1, N_MOD, n), lambda i, j: (grp(i), 0, 0)),
    ]
    args += [w, x, mods]
    return pl.pallas_call(
        functools.partial(_linear_residual_kernel, k_gate=k_gate, split_tiles=split_tiles),
        grid=(n_tiles, 1),
        in_specs=in_specs,
        out_specs=pl.BlockSpec((tm, n), lambda i, j: (i, 0)),
        out_shape=jax.ShapeDtypeStruct((n_tiles * tm, n), F32),
        compiler_params=_cparams("parallel", "arbitrary"),
        name="linear_residual",
    )(*args)


NA_QROWS = 4
NA_QTOK = NA_QROWS * GRID_W
NA_BAND_BLOCKS = 3
NA_PAIRS = 2 * NA_KH


def _na_bias_kernel(rpb_ref, o_ref):
    h = pl.program_id(0)
    cq = lax.broadcasted_iota(jnp.int32, (GRID_W, 2 * GRID_W), 0)
    lane = lax.broadcasted_iota(jnp.int32, (GRID_W, 2 * GRID_W), 1)
    second = lane >= GRID_W
    ck = jnp.where(second, lane - GRID_W, lane)
    n_dc = 2 * NA_KW - 1
    code = ck - cq + (NA_KW - 1) + jnp.where(second, n_dc, 0)
    start = jnp.clip(cq - NA_KW // 2, 0, GRID_W - NA_KW)
    in_win = (ck >= start) & (ck < start + NA_KW)
    for p in range(NA_PAIRS):
        tile = jnp.zeros((GRID_W, 2 * GRID_W), F32)
        for half in range(2):
            dr = p - NA_KH + half
            if not (-(NA_KH - 1) <= dr <= NA_KH - 1):
                continue
            for dc in range(n_dc):
                val = rpb_ref[h, dr + NA_KH - 1, dc]
                tile = jnp.where(code == half * n_dc + dc, val, tile)
        o_ref[0, p] = jnp.where(in_win, tile * LOG2E, NEG_INF)


def na_bias_table(rpb):
    heads = rpb.shape[0]
    return pl.pallas_call(
        _na_bias_kernel,
        grid=(heads,),
        in_specs=[pl.BlockSpec(memory_space=pltpu.SMEM)],
        out_specs=pl.BlockSpec((1, NA_PAIRS, GRID_W, 2 * GRID_W), lambda h: (h, 0, 0, 0)),
        out_shape=jax.ShapeDtypeStruct((heads, NA_PAIRS, GRID_W, 2 * GRID_W), F32),
        compiler_params=_cparams("parallel"),
        name="na_bias_table",
    )(rpb)


def _na_kernel(q_ref, k0_ref, k1_ref, k2_ref, v0_ref, v1_ref, v2_ref, kz_ref, vz_ref,
               t2_ref, o_ref, vm_ref, ob_ref, *, rows):
    qb = pl.program_id(1)
    r0 = NA_QROWS * qb
    u0 = jnp.clip(r0 - NA_KH // 2, 0, rows - NA_BAND_BLOCKS * NA_QROWS)
    k_refs = (k0_ref, k1_ref, k2_ref, kz_ref)
    v_refs = (v0_ref, v1_ref, v2_ref, vz_ref)

    q_row = r0 + (lax.broadcasted_iota(jnp.int32, (NA_QTOK, NA_QTOK), 0) >> 6)
    k_off = lax.broadcasted_iota(jnp.int32, (NA_QTOK, NA_QTOK), 1) >> 6
    rs = jnp.clip(q_row - NA_KH // 2, 0, rows - NA_KH)
    for i in range(NA_BAND_BLOCKS):
        rk = u0 + NA_QROWS * i + k_off
        vm_ref[i] = jnp.where((rk >= rs) & (rk < rs + NA_KH), 0.0, NEG_INF)

    def scores(h):
        return _dot_t(q_ref[h], jnp.concatenate([r[h] for r in k_refs], axis=0))

    def finish_head(h, s_all):
        s = []
        for i in range(NA_BAND_BLOCKS):
            bias_rows = []
            for qi in range(NA_QROWS):
                pair = []
                for c in range(2):
                    dr0 = u0 + NA_QROWS * i + 2 * c - (r0 + qi)
                    pair.append(t2_ref[h, jnp.clip(dr0 + NA_KH, 0, NA_PAIRS - 1)])
                bias_rows.append(jnp.concatenate(pair, axis=1))
            s.append(s_all[:, i * NA_QTOK:(i + 1) * NA_QTOK] + jnp.concatenate(bias_rows, axis=0) + vm_ref[i])
        s.append(s_all[:, NA_BAND_BLOCKS * NA_QTOK:])
        m = jnp.maximum(jnp.maximum(s[0], s[1]), jnp.maximum(s[2], s[3])).max(axis=-1, keepdims=True)
        p = jnp.concatenate([jnp.exp2(si - m) for si in s], axis=1).astype(BF16)
        v = jnp.concatenate([r[h] for r in v_refs], axis=0)
        o = _dot(p, jnp.concatenate([v, jnp.ones_like(v)], axis=1))
        ob_ref[h] = (o[:, :HEAD_DIM] * (1.0 / o[:, HEAD_DIM:])).astype(BF16)

    s_next = scores(0)
    for h in range(NA_HEADS):
        s_cur = s_next
        if h + 1 < NA_HEADS:
            s_next = scores(h + 1)
        finish_head(h, s_cur)
    for h in range(NA_HEADS):
        o_ref[:, h * HEAD_DIM:(h + 1) * HEAD_DIM] = ob_ref[h]


def na_attention(qkv, t2, *, batch, seq, ctx_len):
    rows = seq // GRID_W
    nqb = seq // NA_QTOK
    z0 = batch * seq // ctx_len
    assert ctx_len == NA_QTOK and rows >= NA_BAND_BLOCKS * NA_QROWS
    blk = (NA_HEADS, NA_QTOK, HEAD_DIM)

    def band(i):
        return lambda b, q: (0, b * nqb + jnp.clip(q - 1, 0, nqb - NA_BAND_BLOCKS) + i, 0)

    def shifted(fn, part):
        return lambda b, q: (part,) + fn(b, q)[1:]

    in_specs = [pl.BlockSpec(blk, lambda b, q: (0, b * nqb + q, 0))]
    in_specs += [pl.BlockSpec(blk, shifted(band(i), 1)) for i in range(NA_BAND_BLOCKS)]
    in_specs += [pl.BlockSpec(blk, shifted(band(i), 2)) for i in range(NA_BAND_BLOCKS)]
    in_specs += [pl.BlockSpec(blk, lambda b, q: (1, z0 + b, 0)),
                 pl.BlockSpec(blk, lambda b, q: (2, z0 + b, 0)),
                 pl.BlockSpec(t2.shape, lambda b, q: (0, 0, 0, 0))]
    return pl.pallas_call(
        functools.partial(_na_kernel, rows=rows),
        grid=(batch, nqb),
        in_specs=in_specs,
        out_specs=pl.BlockSpec((NA_QTOK, NA_HEADS * HEAD_DIM), lambda b, q: (b * nqb + q, 0)),
        out_shape=jax.ShapeDtypeStruct((batch * seq, NA_HEADS * HEAD_DIM), BF16),
        scratch_shapes=[pltpu.VMEM((NA_BAND_BLOCKS, NA_QTOK, NA_QTOK), F32),
                        pltpu.VMEM(blk, BF16)],
        compiler_params=_cparams("parallel", "arbitrary"),
        name="na_attention",
    )(*([qkv] * 9), t2)


MLA_TQ = 1024
MLA_TK = 1024
ONES_ROWS = 16


def _mla_kernel(qn_ref, qp_ref, kn_ref, kp_ref, vt_ref, knz_ref, kpz_ref, vtz_ref, o_ref, *, seq):
    q = jnp.concatenate([qn_ref[0], qp_ref[0]], axis=1)
    tq = q.shape[0]

    n_blocks = seq // MLA_TK

    def scores(kb):
        if kb == n_blocks:
            return _dot_t(jnp.concatenate([knz_ref[0], kpz_ref[0]], axis=1), q)
        rows = slice(kb * MLA_TK, (kb + 1) * MLA_TK)
        return _dot_t(jnp.concatenate([kn_ref[0, rows, :], kp_ref[0, rows, :]], axis=1), q)

    m = jnp.full((1, tq), NEG_INF, F32)
    acc = jnp.zeros((MLA_V + ONES_ROWS, tq), F32)
    s_next = scores(0)
    for kb in range(n_blocks + 1):
        s = s_next
        if kb < n_blocks:
            s_next = scores(kb + 1)
            vt = vt_ref[0, :, kb * MLA_TK:(kb + 1) * MLA_TK]
        else:
            vt = vtz_ref[0]
        vt = jnp.concatenate([vt, jnp.ones((ONES_ROWS, vt.shape[1]), BF16)], axis=0)
        m_new = jnp.maximum(m, s.max(axis=0, keepdims=True))
        alpha = jnp.exp2(m - m_new)
        p = jnp.exp2(s - m_new)
        acc = alpha * acc + _dot(vt, p.astype(BF16))
        m = m_new
    l = acc[MLA_V:MLA_V + 1, :]
    o_ref[...] = (acc[:MLA_V, :] * (1.0 / l)).T.astype(BF16)


def mla_attention(qn, qp, kn, kp, vt, *, batch, seq, ctx_len):
    h_ = MLA_HEADS
    nqb = seq // MLA_TQ
    z0 = batch * seq // ctx_len
    qblk = (1, MLA_TQ, LANE)
    kblk = (1, seq, LANE)
    zblk = (1, ctx_len, LANE)
    in_specs = [
        pl.BlockSpec(qblk, lambda b, h, q: (h, b * nqb + q, 0)),
        pl.BlockSpec(qblk, lambda b, h, q: (h, b * nqb + q, 0)),
        pl.BlockSpec(kblk, lambda b, h, q: (h, b, 0)),
        pl.BlockSpec(kblk, lambda b, h, q: (0, b, 0)),
        pl.BlockSpec((1, MLA_V, seq), lambda b, h, q: (h, 0, b)),
        pl.BlockSpec(zblk, lambda b, h, q: (h, z0 + b, 0)),
        pl.BlockSpec(zblk, lambda b, h, q: (0, z0 + b, 0)),
        pl.BlockSpec((1, MLA_V, ctx_len), lambda b, h, q: (h, 0, z0 + b)),
    ]
    return pl.pallas_call(
        functools.partial(_mla_kernel, seq=seq),
        grid=(batch, h_, nqb),
        in_specs=in_specs,
        out_specs=pl.BlockSpec((MLA_TQ, MLA_V), lambda b, h, q: (b * nqb + q, h)),
        out_shape=jax.ShapeDtypeStruct((batch * seq, h_ * MLA_V), BF16),
        compiler_params=_cparams("parallel", "parallel", "arbitrary"),
        name="mla_attention",
    )(qn, qp, kn, kp, vt, kn, kp, vt)


SWA_TQ = SWA_WINDOW


def _swa_kernel(sink_ref, q_ref, k0_ref, k1_ref, k2_ref, v0_ref, v1_ref, v2_ref,
                kz_ref, vz_ref, o_ref, *, n_blocks):
    qb = pl.program_id(1)
    g = SWA_HEADS // SWA_KV_HEADS
    m_rows = g * SWA_TQ
    row = lax.broadcasted_iota(jnp.int32, (m_rows, SWA_TQ), 0) & (SWA_TQ - 1)
    lane = lax.broadcasted_iota(jnp.int32, (m_rows, SWA_TQ), 1)
    mask_prev = (lane >= row) & (qb >= 1)
    mask_next = (lane <= row) & (qb <= n_blocks - 2)
    grp_id = lax.broadcasted_iota(jnp.int32, (m_rows, 1), 0) >> 7
    k_refs = (k0_ref, k1_ref, k2_ref, kz_ref)
    v_refs = (v0_ref, v1_ref, v2_ref, vz_ref)

    def scores(kvh):
        q4 = q_ref[kvh * g:(kvh + 1) * g].reshape(m_rows, HEAD_DIM)
        return _dot_t(q4, jnp.concatenate([r[kvh] for r in k_refs], axis=0))

    s_next = scores(0)
    for kvh in range(SWA_KV_HEADS):
        s_all = s_next
        if kvh + 1 < SWA_KV_HEADS:
            s_next = scores(kvh + 1)
        sink = jnp.zeros((m_rows, 1), F32)
        for gi in range(g):
            sink = jnp.where(grp_id == gi, sink_ref[kvh * g + gi] * LOG2E, sink)
        s = [jnp.where(mask_prev, s_all[:, :SWA_TQ], NEG_INF),
             s_all[:, SWA_TQ:2 * SWA_TQ],
             jnp.where(mask_next, s_all[:, 2 * SWA_TQ:3 * SWA_TQ], NEG_INF)]
        s += [s_all[:, c:c + SWA_TQ] for c in range(3 * SWA_TQ, s_all.shape[1], SWA_TQ)]
        mx = s[0]
        for si in s[1:]:
            mx = jnp.maximum(mx, si)
        m = jnp.maximum(mx.max(axis=-1, keepdims=True), sink)
        p = [jnp.exp2(si - m) for si in s]
        ps = p[0]
        for pi in p[1:]:
            ps = ps + pi
        l = ps.sum(axis=-1, keepdims=True) + jnp.exp2(sink - m)
        o = _dot(jnp.concatenate(p, axis=1).astype(BF16), jnp.concatenate([r[kvh] for r in v_refs], axis=0))
        o = (o * (1.0 / l)).astype(BF16)
        for gi in range(g):
            hq = kvh * g + gi
            o_ref[:, hq * HEAD_DIM:(hq + 1) * HEAD_DIM] = o[gi * SWA_TQ:(gi + 1) * SWA_TQ]


def swa_attention(qkv, sink, *, batch, seq, ctx_len):
    nb = seq // SWA_TQ
    kvh = SWA_KV_HEADS
    qpart = SWA_HEADS // kvh
    z0 = batch * seq // ctx_len
    kblk = (kvh, SWA_TQ, HEAD_DIM)
    zblk = (kvh, ctx_len, HEAD_DIM)
    assert ctx_len % SWA_TQ == 0

    def kmap(part, d):
        return lambda b, q: (part, b * nb + jnp.clip(q + d, 0, nb - 1), 0)

    in_specs = [pl.BlockSpec(memory_space=pltpu.SMEM),
                pl.BlockSpec((SWA_HEADS, SWA_TQ, HEAD_DIM), lambda b, q: (0, b * nb + q, 0))]
    in_specs += [pl.BlockSpec(kblk, kmap(qpart, d)) for d in (-1, 0, 1)]
    in_specs += [pl.BlockSpec(kblk, kmap(qpart + 1, d)) for d in (-1, 0, 1)]
    in_specs += [pl.BlockSpec(zblk, lambda b, q: (qpart, z0 + b, 0)),
                 pl.BlockSpec(zblk, lambda b, q: (qpart + 1, z0 + b, 0))]
    return pl.pallas_call(
        functools.partial(_swa_kernel, n_blocks=nb),
        grid=(batch, nb),
        in_specs=in_specs,
        out_specs=pl.BlockSpec((SWA_TQ, SWA_HEADS * HEAD_DIM), lambda b, q: (b * nb + q, 0)),
        out_shape=jax.ShapeDtypeStruct((batch * seq, SWA_HEADS * HEAD_DIM), BF16),
        compiler_params=_cparams("parallel", "arbitrary"),
        name="swa_attention",
    )(sink, *([qkv] * 9))


def _ctx_kernel(*refs, two_part, use_sink, v_transposed):
    it = iter(refs)
    sink_ref = next(it) if use_sink else None
    if two_part:
        q = jnp.concatenate([next(it)[0], next(it)[0]], axis=1)
        k = jnp.concatenate([next(it)[0], next(it)[0]], axis=1)
    else:
        q = next(it)[0]
        k = next(it)[0]
    v_ref, o_ref = next(it), next(it)
    s = _dot_t(q, k)
    m = s.max(axis=-1, keepdims=True)
    if use_sink:
        sink = sink_ref[pl.program_id(1)] * LOG2E
        m = jnp.maximum(m, sink)
    p = jnp.exp2(s - m)
    l = p.sum(axis=-1, keepdims=True)
    if use_sink:
        l = l + jnp.exp2(sink - m)
    pv = _dot_t(p.astype(BF16), v_ref[0]) if v_transposed else _dot(p.astype(BF16), v_ref[0])
    o_ref[...] = (pv * (1.0 / l)).astype(BF16)


def ctx_attention(q_parts, k_parts, v_part, *, batch, ctx_len, z0, n_heads, sink=None,
                  v_transposed=False):
    blk = (1, ctx_len, LANE)
    in_specs, args = [], []
    if sink is not None:
        in_specs.append(pl.BlockSpec(memory_space=pltpu.SMEM))
        args.append(sink)
    for arr, head_fn in (*q_parts, *k_parts):
        in_specs.append(pl.BlockSpec(blk, functools.partial(
            lambda b, h, head_fn: (head_fn(h), z0 + b, 0), head_fn=head_fn)))
        args.append(arr)
    v_arr, v_head = v_part
    if v_transposed:
        in_specs.append(pl.BlockSpec((1, LANE, ctx_len), lambda b, h: (v_head(h), 0, z0 + b)))
    else:
        in_specs.append(pl.BlockSpec(blk, lambda b, h: (v_head(h), z0 + b, 0)))
    args.append(v_arr)
    return pl.pallas_call(
        functools.partial(_ctx_kernel, two_part=len(q_parts) == 2, use_sink=sink is not None,
                          v_transposed=v_transposed),
        grid=(batch, n_heads),
        in_specs=in_specs,
        out_specs=pl.BlockSpec((ctx_len, LANE), lambda b, h: (b, h)),
        out_shape=jax.ShapeDtypeStruct((batch * ctx_len, n_heads * LANE), BF16),
        compiler_params=_cparams("parallel", "parallel"),
        name="ctx_attention",
    )(*args)


def _rope_angles(n_tokens, rot_dim):
    t = jnp.arange(n_tokens)
    row = (t // GRID_W).astype(F32)
    col = (t % GRID_W).astype(F32)
    n_freq = rot_dim // 4
    inv_freq = ROPE_BASE ** (-jnp.arange(n_freq, dtype=F32) / n_freq)
    ang = jnp.concatenate([row[:, None] * inv_freq, col[:, None] * inv_freq], axis=-1)
    return jnp.cos(ang), jnp.sin(ang)


def _with_identity_tile(tab, fill, tm):
    return jnp.concatenate([tab, jnp.broadcast_to(fill, (tm, LANE))], axis=0)


def swa_rope_tables(seq, scale, tm):
    cos, sin = _rope_angles(seq, HEAD_DIM)
    c = jnp.concatenate([cos, cos], axis=-1)
    s = jnp.concatenate([-sin, sin], axis=-1)
    one = jnp.ones((LANE,), F32)
    zero = jnp.zeros((LANE,), F32)
    ident_c = _with_identity_tile(jnp.ones_like(c), one, tm)
    ident_s = _with_identity_tile(jnp.zeros_like(s), zero, tm)
    c_sets = jnp.stack([_with_identity_tile(c * scale, one * scale, tm), _with_identity_tile(c, one, tm), ident_c])
    s_sets = jnp.stack([_with_identity_tile(s * scale, zero, tm), _with_identity_tile(s, zero, tm), ident_s])
    return c_sets, s_sets


def mla_rope_tables(seq, scale, tm):
    cos, sin = _rope_angles(seq, MLA_ROPE)
    half = MLA_ROPE // 2
    z = jnp.zeros((seq, half), F32)
    c = jnp.concatenate([cos, cos, z, z], axis=-1)
    s1 = jnp.concatenate([-sin, z, z, z], axis=-1)
    s2 = jnp.concatenate([z, sin, z, z], axis=-1)
    one = jnp.ones((LANE,), F32)
    zero = jnp.zeros((LANE,), F32)
    wt = _with_identity_tile
    c_sets = jnp.stack([wt(jnp.ones_like(c) * scale, one * scale, tm), wt(c * scale, one * scale, tm), wt(c, one, tm)])
    s1_sets = jnp.stack([wt(jnp.zeros_like(c), zero, tm), wt(s1 * scale, zero, tm), wt(s1, zero, tm)])
    s2_sets = jnp.stack([wt(jnp.zeros_like(c), zero, tm), wt(s2 * scale, zero, tm), wt(s2, zero, tm)])
    return c_sets, s1_sets, s2_sets


def kernel(x, c, ctx, c_ctx, w_mod, b_mod, norm_g, ffn_w_in, ffn_w_out, na_w_qkv, na_rpb, na_w_o,
           mla_w_down, mla_q_norm_g, mla_w_q_up, mla_kv_norm_g, mla_w_kv_up, mla_w_o,
           swa_w_qkv, swa_sink, swa_w_o, final_norm_g):
    batch, seq, d = x.shape
    ctx_len = ctx.shape[1]
    depth = w_mod.shape[0]
    tf_, tp, to_ = FFN_ROW_TILE, PROJ_ROW_TILE, OUT_ROW_TILE
    nx, nz = batch * seq, batch * ctx_len
    n_groups = batch + 1
    z0 = nx // ctx_len
    assert nx % ctx_len == 0 and all(seq % t == 0 and nz % t == 0 for t in (tf_, tp, to_))
    ffn_rows = dict(tiles_per_group=seq // tf_, n_groups=n_groups)
    proj_rows = dict(tiles_per_group=seq // tp, n_groups=n_groups, n_tiles=(nx + nz) // tp)
    pos_tiles = seq // tp

    n_cond = 8
    cond = jnp.concatenate([c, c_ctx[None, :], jnp.zeros((n_cond - batch - 1, d), F32)], axis=0)
    mods_all = mod_params(cond, w_mod, b_mod).reshape(depth, n_cond, N_MOD, d)

    w_in_all = ffn_w_in.astype(BF16)
    w_out_all = ffn_w_out.astype(BF16)
    qk_scale = HEAD_DIM ** -0.5 * LOG2E
    mla_scale = (MLA_NOPE + MLA_ROPE) ** -0.5 * LOG2E

    stream = None
    for li in range(depth):
        last = li == depth - 1
        j = li // N_MIXERS
        kind = li % N_MIXERS
        mods = mods_all[li]

        if li == 0:
            stream = ffn_half_step(x.reshape(nx, d), norm_g[li, 0], mods, w_in_all, w_out_all, (li, 0), k0=0,
                                   n_tiles=(nx + nz) // tf_, x_tail=ctx.reshape(nz, d), **ffn_rows)
        else:
            stream = ffn_half_step(stream, norm_g[li, 0], mods, w_in_all, w_out_all, (li, 0), k0=0,
                                   n_tiles=(nx + nz) // tf_, **ffn_rows)

        proj = dict(mods=mods, mod_idx=(3, 4), **proj_rows)
        if kind == 0:
            col_scale = jnp.concatenate([jnp.full((NA_HEADS * HEAD_DIM,), qk_scale, F32),
                                         jnp.ones((2 * NA_HEADS * HEAD_DIM,), F32)])
            qkv = norm_linear(stream, 0, d, norm_g[li, 1], na_w_qkv[j].astype(BF16),
                              col_scale=col_scale, tn=1024, **proj)
            t2 = na_bias_table(na_rpb[j])
            ox = na_attention(qkv, t2, batch=batch, seq=seq, ctx_len=ctx_len)
            if not last:
                hd = lambda part: (qkv, lambda h: part * NA_HEADS + h)
                oz = ctx_attention([hd(0)], [hd(1)], hd(2), batch=batch, ctx_len=ctx_len,
                                   z0=z0, n_heads=NA_HEADS)
            w_o = na_w_o[j]
        elif kind == 1:
            w_down = mla_w_down[j]
            lora = MLA_Q_LORA + MLA_KV_LORA
            cqkv = norm_linear(stream, 0, d, norm_g[li, 1], w_down[:, :lora].astype(BF16),
                               layout="rows", out_dtype=F32, tn=lora, **proj)
            c_t, s1_t, s2_t = mla_rope_tables(seq, mla_scale, tp)
            w_kpe = jnp.pad(w_down[:, lora:], ((0, 0), (0, LANE - MLA_ROPE))).astype(BF16)
            kp = norm_linear(stream, 0, d, norm_g[li, 1], w_kpe, rope=(96, 32),
                             tables=(c_t[2:], s1_t[2:], s2_t[2:]), pos_tiles=pos_tiles,
                             tn=LANE, **proj)
            wq = mla_w_q_up[j].reshape(MLA_Q_LORA, MLA_HEADS, MLA_NOPE + MLA_ROPE)
            wq_n = wq[:, :, :MLA_NOPE].reshape(MLA_Q_LORA, MLA_HEADS * MLA_NOPE).astype(BF16)
            wq_p = jnp.pad(wq[:, :, MLA_NOPE:], ((0, 0), (0, 0), (0, LANE - MLA_ROPE)))
            wq_p = wq_p.reshape(MLA_Q_LORA, MLA_HEADS * LANE).astype(BF16)
            qn = norm_linear(cqkv, 0, MLA_Q_LORA, mla_q_norm_g[j], wq_n,
                             col_scale=jnp.full((MLA_HEADS * MLA_NOPE,), mla_scale, F32), tn=1024, **proj_rows)
            qp = norm_linear(cqkv, 0, MLA_Q_LORA, mla_q_norm_g[j], wq_p, rope=(96, 32),
                             tables=(c_t[1:2], s1_t[1:2], s2_t[1:2]), pos_tiles=pos_tiles, tn=512, **proj_rows)
            wkv = mla_w_kv_up[j].reshape(MLA_KV_LORA, MLA_HEADS, MLA_NOPE + MLA_V)
            w_kn = wkv[:, :, :MLA_NOPE].reshape(MLA_KV_LORA, -1).astype(BF16)
            w_v = wkv[:, :, MLA_NOPE:].reshape(MLA_KV_LORA, -1).astype(BF16)
            kn = norm_linear(cqkv, 1, MLA_KV_LORA, mla_kv_norm_g[j], w_kn, tn=1024, **proj_rows)
            vt = norm_linear(cqkv, 1, MLA_KV_LORA, mla_kv_norm_g[j], w_v, layout="heads_t", tn=1024, **proj_rows)
            ox = mla_attention(qn, qp, kn, kp, vt, batch=batch, seq=seq, ctx_len=ctx_len)
            if not last:
                oz = ctx_attention([(qn, lambda h: h), (qp, lambda h: h)],
                                   [(kn, lambda h: h), (kp, lambda h: 0)], (vt, lambda h: h),
                                   batch=batch, ctx_len=ctx_len, z0=z0, n_heads=MLA_HEADS, v_transposed=True)
            w_o = mla_w_o[j]
        else:
            c_t, s_t = swa_rope_tables(seq, qk_scale, tp)
            n_q_blocks = SWA_HEADS * HEAD_DIM // 512
            qkv = norm_linear(stream, 0, d, norm_g[li, 1], swa_w_qkv[j].astype(BF16), rope=(64,),
                              tables=(c_t, s_t), set_thresholds=(n_q_blocks, n_q_blocks + 1),
                              pos_tiles=pos_tiles, tn=512, **proj)
            ox = swa_attention(qkv, swa_sink[j], batch=batch, seq=seq, ctx_len=ctx_len)
            if not last:
                g = SWA_HEADS // SWA_KV_HEADS
                oz = ctx_attention([(qkv, lambda h: h)], [(qkv, lambda h: SWA_HEADS + h // g)],
                                   (qkv, lambda h: SWA_HEADS + SWA_KV_HEADS + h // g),
                                   batch=batch, ctx_len=ctx_len, z0=z0, n_heads=SWA_HEADS,
                                   sink=swa_sink[j])
            w_o = swa_w_o[j]

        n_rows = nx if last else nx + nz
        stream = linear_residual(ox, w_o.astype(BF16), stream, mods, k_gate=5, n_tiles=n_rows // to_,
                                 a_tail=None if last else oz, tiles_per_group=seq // to_, n_groups=n_groups)
        stream = ffn_half_step(stream, norm_g[li, 2], mods, w_in_all, w_out_all, (li, 1), k0=6,
                               n_tiles=n_rows // tf_, final_gain=final_norm_g if last else None, **ffn_rows)

    return stream.reshape(batch, seq, d)
```

```python
import functools
import math

import jax
import jax.numpy as jnp
from jax import lax
from jax.experimental import pallas as pl
from jax.experimental.pallas import tpu as pltpu

DEPTH = 4
GRID_W = 64
N_MIXERS = 3
N_MOD = 9
RMS_EPS = 1e-6
ROPE_BASE = 10000.0
NEG_INF = -1e30
LOG2E = math.log2(math.e)

NA_HEADS = 16
NA_KH = 8
NA_KW = 16
HEAD_DIM = 128

MLA_HEADS = 16
MLA_Q_LORA = 512
MLA_KV_LORA = 512
MLA_NOPE = 128
MLA_ROPE = 64
MLA_V = 128

SWA_HEADS = 16
SWA_KV_HEADS = 4
SWA_WINDOW = 128

LANE = 128
FFN_ROW_TILE = 1024
FFN_VMEM_LIMIT = 62 * 1024 * 1024
OUT_ROW_TILE = 512
PROJ_ROW_TILE = 1024
FFN_CHUNK = 512
VMEM_LIMIT = 56 * 1024 * 1024

F32 = jnp.float32
BF16 = jnp.bfloat16


def _cparams(*sem):
    return pltpu.CompilerParams(dimension_semantics=sem, vmem_limit_bytes=VMEM_LIMIT)


def _dot(a, b):
    return jnp.dot(a, b, preferred_element_type=F32)


def _dot_t(a, b):
    return lax.dot_general(a, b, (((1,), (1,)), ((), ())), preferred_element_type=F32)


def _sigmoid(x):
    return 1.0 / (1.0 + jnp.exp(-x))


def _rmsnorm_rows(x, g):
    ms = jnp.mean(x * x, axis=-1, keepdims=True)
    return x * lax.rsqrt(ms + RMS_EPS) * g


def _modulated_norm(x, g_ref, mod_ref, k_shift, k_scale):
    r = lax.rsqrt(jnp.mean(x * x, axis=-1, keepdims=True) + RMS_EPS)
    gs = g_ref[...] * (1.0 + mod_ref[0, k_scale:k_scale + 1, :])
    return (x * r) * gs + mod_ref[0, k_shift:k_shift + 1, :]


def _group_fn(tiles_per_group, n_groups):
    return lambda i: jnp.minimum(i // tiles_per_group, n_groups - 1)


def _two_source_specs(block, split_tiles):
    return [pl.BlockSpec(block, lambda i, j: (jnp.minimum(i, split_tiles - 1), 0)),
            pl.BlockSpec(block, lambda i, j: (jnp.maximum(i - split_tiles, 0), 0))]


def _read_rows(ref, tail_ref, split_tiles):
    if tail_ref is None:
        return ref[...]
    return jnp.where(pl.program_id(0) < split_tiles, ref[...], tail_ref[...])


def _mod_kernel(c_ref, w_ref, b_ref, o_ref):
    c = c_ref[...]
    a = (c * _sigmoid(c)).astype(BF16)
    o_ref[0] = _dot(a, w_ref[0].astype(BF16)) + b_ref[0]


def mod_params(cond, w_mod, b_mod):
    depth, d, n = w_mod.shape
    g = cond.shape[0]
    tn = math.gcd(n, 1024)
    return pl.pallas_call(
        _mod_kernel,
        grid=(depth, n // tn),
        in_specs=[
            pl.BlockSpec((g, d), lambda l, j: (0, 0)),
            pl.BlockSpec((1, d, tn), lambda l, j: (l, 0, j)),
            pl.BlockSpec((1, 1, tn), lambda l, j: (l, 0, j)),
        ],
        out_specs=pl.BlockSpec((1, g, tn), lambda l, j: (l, 0, j)),
        out_shape=jax.ShapeDtypeStruct((depth, g, n), F32),
        compiler_params=_cparams("parallel", "parallel"),
        name="mod_params",
    )(cond, w_mod, b_mod.reshape(depth, 1, n))


def _ffn_kernel(*refs, k0, final_norm, split_tiles):
    it = iter(refs)
    x_ref = next(it)
    xt_ref = next(it) if split_tiles is not None else None
    g_ref, mod_ref, wg_ref, wu_ref, wo_ref = (next(it) for _ in range(5))
    gf_ref = next(it) if final_norm else None
    o_ref, h_ref = next(it), next(it)
    j = pl.program_id(1)

    @pl.when(j == 0)
    def _():
        x = _read_rows(x_ref, xt_ref, split_tiles)
        h_ref[...] = _modulated_norm(x, g_ref, mod_ref, k0, k0 + 1).astype(BF16)
        o_ref[...] = jnp.zeros_like(o_ref)

    h = h_ref[...]
    gate = _dot(h, wg_ref[...])
    up = _dot(h, wu_ref[...])
    a = (gate * _sigmoid(gate) * up).astype(BF16)
    o_ref[...] += _dot(a, wo_ref[...])

    @pl.when(j == pl.num_programs(1) - 1)
    def _():
        x = _read_rows(x_ref, xt_ref, split_tiles)
        y = x + (0.5 * mod_ref[0, k0 + 2:k0 + 3, :]) * o_ref[...]
        if final_norm:
            y = _rmsnorm_rows(y, gf_ref[...])
        o_ref[...] = y


def ffn_half_step(x, gain, mods, w_in, w_out, lead, *, k0, n_tiles, tiles_per_group, n_groups,
                  x_tail=None, final_gain=None, tm=FFN_ROW_TILE, tf=FFN_CHUNK):
    d = x.shape[1]
    f = w_out.shape[-2]
    nf = f // tf
    grp = _group_fn(tiles_per_group, n_groups)
    split_tiles = None if x_tail is None else x.shape[0] // tm
    sq = (None,) * len(lead)
    if x_tail is None:
        in_specs, args = [pl.BlockSpec((tm, d), lambda i, j: (i, 0))], [x]
    else:
        in_specs, args = _two_source_specs((tm, d), split_tiles), [x, x_tail]
    in_specs += [
        pl.BlockSpec((1, d), lambda i, j: (0, 0)),
        pl.BlockSpec((1, N_MOD, d), lambda i, j: (grp(i), 0, 0)),
        pl.BlockSpec(sq + (d, tf), lambda i, j: lead + (0, j)),
        pl.BlockSpec(sq + (d, tf), lambda i, j: lead + (0, j + nf)),
        pl.BlockSpec(sq + (tf, d), lambda i, j: lead + (j, 0)),
    ]
    args += [gain.reshape(1, d), mods, w_in, w_in, w_out]
    if final_gain is not None:
        in_specs.append(pl.BlockSpec((1, d), lambda i, j: (0, 0)))
        args.append(final_gain.reshape(1, d))
    return pl.pallas_call(
        functools.partial(_ffn_kernel, k0=k0, final_norm=final_gain is not None, split_tiles=split_tiles),
        grid=(n_tiles, nf),
        in_specs=in_specs,
        out_specs=pl.BlockSpec((tm, d), lambda i, j: (i, 0)),
        out_shape=jax.ShapeDtypeStruct((n_tiles * tm, d), F32),
        scratch_shapes=[pltpu.VMEM((tm, d), BF16)],
        compiler_params=pltpu.CompilerParams(dimension_semantics=("parallel", "arbitrary"),
                                             vmem_limit_bytes=FFN_VMEM_LIMIT),
        name="ffn_half_step",
    )(*args)


def _norm_linear_kernel(*refs, mod_idx, rope, layout, n_chunks):
    it = iter(refs)
    x_ref, g_ref = next(it), next(it)
    mod_ref = next(it) if mod_idx is not None else None
    w_ref = next(it)
    if rope is None:
        cs_ref = next(it)
    else:
        tab_refs = [next(it) for _ in range(len(rope) + 1)]
    o_ref, h_ref = next(it), next(it)

    @pl.when(pl.program_id(1) == 0)
    def _():
        if mod_idx is None:
            y = _rmsnorm_rows(x_ref[...], g_ref[...])
        else:
            y = _modulated_norm(x_ref[...], g_ref, mod_ref, *mod_idx)
        h_ref[...] = y.astype(BF16)

    acc = _dot(h_ref[...], w_ref[...])
    if rope is None:
        acc = acc * cs_ref[...]
    for c in range(n_chunks):
        a = acc[:, c * LANE:(c + 1) * LANE]
        if rope is not None:
            out = a * tab_refs[0][0]
            for shift, tab in zip(rope, tab_refs[1:]):
                out = out + pltpu.roll(a, shift, 1) * tab[0]
            a = out
        if layout == "heads":
            o_ref[c] = a.astype(o_ref.dtype)
        elif layout == "heads_t":
            o_ref[c] = a.T.astype(o_ref.dtype)
        else:
            o_ref[:, c * LANE:(c + 1) * LANE] = a.astype(o_ref.dtype)


def norm_linear(x, x_col_block, k, gain, w, *, n_tiles, tiles_per_group, n_groups,
                mods=None, mod_idx=None, col_scale=None, rope=None, tables=None,
                set_thresholds=(), pos_tiles=None, layout="heads", out_dtype=BF16,
                tm=PROJ_ROW_TILE, tn=512):
    n = w.shape[1]
    nj = n // tn
    n_chunks = tn // LANE
    grp = _group_fn(tiles_per_group, n_groups)
    in_specs = [
        pl.BlockSpec((tm, k), lambda i, j: (i, x_col_block)),
        pl.BlockSpec((1, k), lambda i, j: (0, 0)),
    ]
    args = [x, gain.reshape(1, k)]
    if mod_idx is not None:
        in_specs.append(pl.BlockSpec((1, N_MOD, k), lambda i, j: (grp(i), 0, 0)))
        args.append(mods)
    in_specs.append(pl.BlockSpec((k, tn), lambda i, j: (0, j)))
    args.append(w)
    if rope is None:
        if col_scale is None:
            col_scale = jnp.ones((n,), F32)
        in_specs.append(pl.BlockSpec((1, tn), lambda i, j: (0, j)))
        args.append(col_scale.reshape(1, n))
    else:
        n_lat_tiles = tiles_per_group * (n_groups - 1)

        def tab_map(i, j):
            s = sum((j >= t).astype(jnp.int32) for t in set_thresholds) if set_thresholds else 0
            p = jnp.where(i < n_lat_tiles, i % pos_tiles, pos_tiles)
            return (s, p, 0)

        for t in tables:
            in_specs.append(pl.BlockSpec((1, tm, LANE), tab_map))
            args.append(t)
    rows = n_tiles * tm
    if layout == "heads":
        out_spec = pl.BlockSpec((n_chunks, tm, LANE), lambda i, j: (j, i, 0))
        out_shape = jax.ShapeDtypeStruct((n // LANE, rows, LANE), out_dtype)
    elif layout == "heads_t":
        out_spec = pl.BlockSpec((n_chunks, LANE, tm), lambda i, j: (j, 0, i))
        out_shape = jax.ShapeDtypeStruct((n // LANE, LANE, rows), out_dtype)
    else:
        out_spec = pl.BlockSpec((tm, tn), lambda i, j: (i, j))
        out_shape = jax.ShapeDtypeStruct((rows, n), out_dtype)
    return pl.pallas_call(
        functools.partial(_norm_linear_kernel, mod_idx=mod_idx, rope=rope,
                          layout=layout, n_chunks=n_chunks),
        grid=(n_tiles, nj),
        in_specs=in_specs,
        out_specs=out_spec,
        out_shape=out_shape,
        scratch_shapes=[pltpu.VMEM((tm, k), BF16)],
        compiler_params=_cparams("parallel", "arbitrary"),
        name="norm_linear",
    )(*args)


def _linear_residual_kernel(*refs, k_gate, split_tiles):
    it = iter(refs)
    a_ref = next(it)
    at_ref = next(it) if split_tiles is not None else None
    w_ref, x_ref, mod_ref, o_ref = next(it), next(it), next(it), next(it)
    acc = _dot(_read_rows(a_ref, at_ref, split_tiles), w_ref[...])
    o_ref[...] = x_ref[...] + mod_ref[0, k_gate:k_gate + 1, :] * acc


def linear_residual(a, w, x, mods, *, k_gate, n_tiles, tiles_per_group, n_groups, a_tail=None,
                    tm=OUT_ROW_TILE):
    kdim, n = w.shape
    grp = _group_fn(tiles_per_group, n_groups)
    split_tiles = None if a_tail is None else a.shape[0] // tm
    if a_tail is None:
        in_specs, args = [pl.BlockSpec((tm, kdim), lambda i, j: (i, 0))], [a]
    else:
        in_specs, args = _two_source_specs((tm, kdim), split_tiles), [a, a_tail]
    in_specs += [
        pl.BlockSpec((kdim, n), lambda i, j: (0, 0)),
        pl.BlockSpec((tm, n), lambda i, j: (i, 0)),
        pl.BlockSpec((1, N_MOD, n), lambda i, j: (grp(i), 0, 0)),
    ]
    args += [w, x, mods]
    return pl.pallas_call(
        functools.partial(_linear_residual_kernel, k_gate=k_gate, split_tiles=split_tiles),
        grid=(n_tiles, 1),
        in_specs=in_specs,
        out_specs=pl.BlockSpec((tm, n), lambda i, j: (i, 0)),
        out_shape=jax.ShapeDtypeStruct((n_tiles * tm, n), F32),
        compiler_params=_cparams("parallel", "arbitrary"),
        name="linear_residual",
    )(*args)


NA_QROWS = 4
NA_QTOK = NA_QROWS * GRID_W
NA_BAND_BLOCKS = 3
NA_PAIRS = 2 * NA_KH


def _na_bias_kernel(rpb_ref, o_ref):
    h = pl.program_id(0)
    cq = lax.broadcasted_iota(jnp.int32, (GRID_W, 2 * GRID_W), 0)
    lane = lax.broadcasted_iota(jnp.int32, (GRID_W, 2 * GRID_W), 1)
    second = lane >= GRID_W
    ck = jnp.where(second, lane - GRID_W, lane)
    n_dc = 2 * NA_KW - 1
    code = ck - cq + (NA_KW - 1) + jnp.where(second, n_dc, 0)
    start = jnp.clip(cq - NA_KW // 2, 0, GRID_W - NA_KW)
    in_win = (ck >= start) & (ck < start + NA_KW)
    for p in range(NA_PAIRS):
        tile = jnp.zeros((GRID_W, 2 * GRID_W), F32)
        for half in range(2):
            dr = p - NA_KH + half
            if not (-(NA_KH - 1) <= dr <= NA_KH - 1):
                continue
            for dc in range(n_dc):
                val = rpb_ref[h, dr + NA_KH - 1, dc]
                tile = jnp.where(code == half * n_dc + dc, val, tile)
        o_ref[0, p] = jnp.where(in_win, tile * LOG2E, NEG_INF)


def na_bias_table(rpb):
    heads = rpb.shape[0]
    return pl.pallas_call(
        _na_bias_kernel,
        grid=(heads,),
        in_specs=[pl.BlockSpec(memory_space=pltpu.SMEM)],
        out_specs=pl.BlockSpec((1, NA_PAIRS, GRID_W, 2 * GRID_W), lambda h: (h, 0, 0, 0)),
        out_shape=jax.ShapeDtypeStruct((heads, NA_PAIRS, GRID_W, 2 * GRID_W), F32),
        compiler_params=_cparams("parallel"),
        name="na_bias_table",
    )(rpb)


def _na_kernel(q_ref, k0_ref, k1_ref, k2_ref, v0_ref, v1_ref, v2_ref, kz_ref, vz_ref,
               t2_ref, o_ref, vm_ref, ob_ref, *, rows):
    qb = pl.program_id(1)
    r0 = NA_QROWS * qb
    u0 = jnp.clip(r0 - NA_KH // 2, 0, rows - NA_BAND_BLOCKS * NA_QROWS)
    k_refs = (k0_ref, k1_ref, k2_ref, kz_ref)
    v_refs = (v0_ref, v1_ref, v2_ref, vz_ref)

    q_row = r0 + (lax.broadcasted_iota(jnp.int32, (NA_QTOK, NA_QTOK), 0) >> 6)
    k_off = lax.broadcasted_iota(jnp.int32, (NA_QTOK, NA_QTOK), 1) >> 6
    rs = jnp.clip(q_row - NA_KH // 2, 0, rows - NA_KH)
    for i in range(NA_BAND_BLOCKS):
        rk = u0 + NA_QROWS * i + k_off
        vm_ref[i] = jnp.where((rk >= rs) & (rk < rs + NA_KH), 0.0, NEG_INF)

    def scores(h):
        return _dot_t(q_ref[h], jnp.concatenate([r[h] for r in k_refs], axis=0))

    def finish_head(h, s_all):
        s = []
        for i in range(NA_BAND_BLOCKS):
            bias_rows = []
            for qi in range(NA_QROWS):
                pair = []
                for c in range(2):
                    dr0 = u0 + NA_QROWS * i + 2 * c - (r0 + qi)
                    pair.append(t2_ref[h, jnp.clip(dr0 + NA_KH, 0, NA_PAIRS - 1)])
                bias_rows.append(jnp.concatenate(pair, axis=1))
            s.append(s_all[:, i * NA_QTOK:(i + 1) * NA_QTOK] + jnp.concatenate(bias_rows, axis=0) + vm_ref[i])
        s.append(s_all[:, NA_BAND_BLOCKS * NA_QTOK:])
        m = jnp.maximum(jnp.maximum(s[0], s[1]), jnp.maximum(s[2], s[3])).max(axis=-1, keepdims=True)
        p = jnp.concatenate([jnp.exp2(si - m) for si in s], axis=1).astype(BF16)
        v = jnp.concatenate([r[h] for r in v_refs], axis=0)
        o = _dot(p, jnp.concatenate([v, jnp.ones_like(v)], axis=1))
        ob_ref[h] = (o[:, :HEAD_DIM] * (1.0 / o[:, HEAD_DIM:])).astype(BF16)

    s_next = scores(0)
    for h in range(NA_HEADS):
        s_cur = s_next
        if h + 1 < NA_HEADS:
            s_next = scores(h + 1)
        finish_head(h, s_cur)
    for h in range(NA_HEADS):
        o_ref[:, h * HEAD_DIM:(h + 1) * HEAD_DIM] = ob_ref[h]


def na_attention(qkv, t2, *, batch, seq, ctx_len):
    rows = seq // GRID_W
    nqb = seq // NA_QTOK
    z0 = batch * seq // ctx_len
    assert ctx_len == NA_QTOK and rows >= NA_BAND_BLOCKS * NA_QROWS
    blk = (NA_HEADS, NA_QTOK, HEAD_DIM)

    def band(i):
        return lambda b, q: (0, b * nqb + jnp.clip(q - 1, 0, nqb - NA_BAND_BLOCKS) + i, 0)

    def shifted(fn, part):
        return lambda b, q: (part,) + fn(b, q)[1:]

    in_specs = [pl.BlockSpec(blk, lambda b, q: (0, b * nqb + q, 0))]
    in_specs += [pl.BlockSpec(blk, shifted(band(i), 1)) for i in range(NA_BAND_BLOCKS)]
    in_specs += [pl.BlockSpec(blk, shifted(band(i), 2)) for i in range(NA_BAND_BLOCKS)]
    in_specs += [pl.BlockSpec(blk, lambda b, q: (1, z0 + b, 0)),
                 pl.BlockSpec(blk, lambda b, q: (2, z0 + b, 0)),
                 pl.BlockSpec(t2.shape, lambda b, q: (0, 0, 0, 0))]
    return pl.pallas_call(
        functools.partial(_na_kernel, rows=rows),
        grid=(batch, nqb),
        in_specs=in_specs,
        out_specs=pl.BlockSpec((NA_QTOK, NA_HEADS * HEAD_DIM), lambda b, q: (b * nqb + q, 0)),
        out_shape=jax.ShapeDtypeStruct((batch * seq, NA_HEADS * HEAD_DIM), BF16),
        scratch_shapes=[pltpu.VMEM((NA_BAND_BLOCKS, NA_QTOK, NA_QTOK), F32),
                        pltpu.VMEM(blk, BF16)],
        compiler_params=_cparams("parallel", "arbitrary"),
        name="na_attention",
    )(*([qkv] * 9), t2)


MLA_TQ = 1024
MLA_TK = 1024


def _mla_kernel(qn_ref, qp_ref, kn_ref, kp_ref, vt_ref, knz_ref, kpz_ref, vtz_ref, o_ref, *, seq):
    q = jnp.concatenate([qn_ref[0], qp_ref[0]], axis=1)
    tq = q.shape[0]

    n_blocks = seq // MLA_TK

    def scores(kb):
        if kb == n_blocks:
            return _dot_t(jnp.concatenate([knz_ref[0], kpz_ref[0]], axis=1), q)
        rows = slice(kb * MLA_TK, (kb + 1) * MLA_TK)
        return _dot_t(jnp.concatenate([kn_ref[0, rows, :], kp_ref[0, rows, :]], axis=1), q)

    m = jnp.full((1, tq), NEG_INF, F32)
    l = jnp.zeros((1, tq), F32)
    acc = jnp.zeros((MLA_V, tq), F32)
    s_next = scores(0)
    for kb in range(n_blocks + 1):
        s = s_next
        if kb < n_blocks:
            s_next = scores(kb + 1)
            vt = vt_ref[0, :, kb * MLA_TK:(kb + 1) * MLA_TK]
        else:
            vt = vtz_ref[0]
        m_new = jnp.maximum(m, s.max(axis=0, keepdims=True))
        alpha = jnp.exp2(m - m_new)
        p = jnp.exp2(s - m_new)
        l = alpha * l + p.sum(axis=0, keepdims=True)
        acc = alpha * acc + _dot(vt, p.astype(BF16))
        m = m_new
    o_ref[...] = (acc * (1.0 / l)).T.astype(BF16)


def mla_attention(qn, qp, kn, kp, vt, *, batch, seq, ctx_len):
    h_ = MLA_HEADS
    nqb = seq // MLA_TQ
    z0 = batch * seq // ctx_len
    qblk = (1, MLA_TQ, LANE)
    kblk = (1, seq, LANE)
    zblk = (1, ctx_len, LANE)
    in_specs = [
        pl.BlockSpec(qblk, lambda b, h, q: (h, b * nqb + q, 0)),
        pl.BlockSpec(qblk, lambda b, h, q: (h, b * nqb + q, 0)),
        pl.BlockSpec(kblk, lambda b, h, q: (h, b, 0)),
        pl.BlockSpec(kblk, lambda b, h, q: (0, b, 0)),
        pl.BlockSpec((1, MLA_V, seq), lambda b, h, q: (h, 0, b)),
        pl.BlockSpec(zblk, lambda b, h, q: (h, z0 + b, 0)),
        pl.BlockSpec(zblk, lambda b, h, q: (0, z0 + b, 0)),
        pl.BlockSpec((1, MLA_V, ctx_len), lambda b, h, q: (h, 0, z0 + b)),
    ]
    return pl.pallas_call(
        functools.partial(_mla_kernel, seq=seq),
        grid=(batch, h_, nqb),
        in_specs=in_specs,
        out_specs=pl.BlockSpec((MLA_TQ, MLA_V), lambda b, h, q: (b * nqb + q, h)),
        out_shape=jax.ShapeDtypeStruct((batch * seq, h_ * MLA_V), BF16),
        compiler_params=_cparams("parallel", "parallel", "arbitrary"),
        name="mla_attention",
    )(qn, qp, kn, kp, vt, kn, kp, vt)


SWA_TQ = SWA_WINDOW


def _swa_kernel(sink_ref, q_ref, k0_ref, k1_ref, k2_ref, v0_ref, v1_ref, v2_ref,
                kz_ref, vz_ref, o_ref, *, n_blocks):
    qb = pl.program_id(1)
    g = SWA_HEADS // SWA_KV_HEADS
    m_rows = g * SWA_TQ
    row = lax.broadcasted_iota(jnp.int32, (m_rows, SWA_TQ), 0) & (SWA_TQ - 1)
    lane = lax.broadcasted_iota(jnp.int32, (m_rows, SWA_TQ), 1)
    mask_prev = (lane >= row) & (qb >= 1)
    mask_next = (lane <= row) & (qb <= n_blocks - 2)
    grp_id = lax.broadcasted_iota(jnp.int32, (m_rows, 1), 0) >> 7
    k_refs = (k0_ref, k1_ref, k2_ref, kz_ref)
    v_refs = (v0_ref, v1_ref, v2_ref, vz_ref)

    def scores(kvh):
        q4 = q_ref[kvh * g:(kvh + 1) * g].reshape(m_rows, HEAD_DIM)
        return _dot_t(q4, jnp.concatenate([r[kvh] for r in k_refs], axis=0))

    s_next = scores(0)
    for kvh in range(SWA_KV_HEADS):
        s_all = s_next
        if kvh + 1 < SWA_KV_HEADS:
            s_next = scores(kvh + 1)
        sink = jnp.zeros((m_rows, 1), F32)
        for gi in range(g):
            sink = jnp.where(grp_id == gi, sink_ref[kvh * g + gi] * LOG2E, sink)
        s = [jnp.where(mask_prev, s_all[:, :SWA_TQ], NEG_INF),
             s_all[:, SWA_TQ:2 * SWA_TQ],
             jnp.where(mask_next, s_all[:, 2 * SWA_TQ:3 * SWA_TQ], NEG_INF)]
        s += [s_all[:, c:c + SWA_TQ] for c in range(3 * SWA_TQ, s_all.shape[1], SWA_TQ)]
        mx = s[0]
        for si in s[1:]:
            mx = jnp.maximum(mx, si)
        m = jnp.maximum(mx.max(axis=-1, keepdims=True), sink)
        p = [jnp.exp2(si - m) for si in s]
        ps = p[0]
        for pi in p[1:]:
            ps = ps + pi
        l = ps.sum(axis=-1, keepdims=True) + jnp.exp2(sink - m)
        o = _dot(jnp.concatenate(p, axis=1).astype(BF16), jnp.concatenate([r[kvh] for r in v_refs], axis=0))
        o = (o * (1.0 / l)).astype(BF16)
        for gi in range(g):
            hq = kvh * g + gi
            o_ref[:, hq * HEAD_DIM:(hq + 1) * HEAD_DIM] = o[gi * SWA_TQ:(gi + 1) * SWA_TQ]


def swa_attention(qkv, sink, *, batch, seq, ctx_len):
    nb = seq // SWA_TQ
    kvh = SWA_KV_HEADS
    qpart = SWA_HEADS // kvh
    z0 = batch * seq // ctx_len
    kblk = (kvh, SWA_TQ, HEAD_DIM)
    zblk = (kvh, ctx_len, HEAD_DIM)
    assert ctx_len % SWA_TQ == 0

    def kmap(part, d):
        return lambda b, q: (part, b * nb + jnp.clip(q + d, 0, nb - 1), 0)

    in_specs = [pl.BlockSpec(memory_space=pltpu.SMEM),
                pl.BlockSpec((SWA_HEADS, SWA_TQ, HEAD_DIM), lambda b, q: (0, b * nb + q, 0))]
    in_specs += [pl.BlockSpec(kblk, kmap(qpart, d)) for d in (-1, 0, 1)]
    in_specs += [pl.BlockSpec(kblk, kmap(qpart + 1, d)) for d in (-1, 0, 1)]
    in_specs += [pl.BlockSpec(zblk, lambda b, q: (qpart, z0 + b, 0)),
                 pl.BlockSpec(zblk, lambda b, q: (qpart + 1, z0 + b, 0))]
    return pl.pallas_call(
        functools.partial(_swa_kernel, n_blocks=nb),
        grid=(batch, nb),
        in_specs=in_specs,
        out_specs=pl.BlockSpec((SWA_TQ, SWA_HEADS * HEAD_DIM), lambda b, q: (b * nb + q, 0)),
        out_shape=jax.ShapeDtypeStruct((batch * seq, SWA_HEADS * HEAD_DIM), BF16),
        compiler_params=_cparams("parallel", "arbitrary"),
        name="swa_attention",
    )(sink, *([qkv] * 9))


def _ctx_kernel(*refs, two_part, use_sink, v_transposed):
    it = iter(refs)
    sink_ref = next(it) if use_sink else None
    if two_part:
        q = jnp.concatenate([next(it)[0], next(it)[0]], axis=1)
        k = jnp.concatenate([next(it)[0], next(it)[0]], axis=1)
    else:
        q = next(it)[0]
        k = next(it)[0]
    v_ref, o_ref = next(it), next(it)
    s = _dot_t(q, k)
    m = s.max(axis=-1, keepdims=True)
    if use_sink:
        sink = sink_ref[pl.program_id(1)] * LOG2E
        m = jnp.maximum(m, sink)
    p = jnp.exp2(s - m)
    l = p.sum(axis=-1, keepdims=True)
    if use_sink:
        l = l + jnp.exp2(sink - m)
    pv = _dot_t(p.astype(BF16), v_ref[0]) if v_transposed else _dot(p.astype(BF16), v_ref[0])
    o_ref[...] = (pv * (1.0 / l)).astype(BF16)


def ctx_attention(q_parts, k_parts, v_part, *, batch, ctx_len, z0, n_heads, sink=None,
                  v_transposed=False):
    blk = (1, ctx_len, LANE)
    in_specs, args = [], []
    if sink is not None:
        in_specs.append(pl.BlockSpec(memory_space=pltpu.SMEM))
        args.append(sink)
    for arr, head_fn in (*q_parts, *k_parts):
        in_specs.append(pl.BlockSpec(blk, functools.partial(
            lambda b, h, head_fn: (head_fn(h), z0 + b, 0), head_fn=head_fn)))
        args.append(arr)
    v_arr, v_head = v_part
    if v_transposed:
        in_specs.append(pl.BlockSpec((1, LANE, ctx_len), lambda b, h: (v_head(h), 0, z0 + b)))
    else:
        in_specs.append(pl.BlockSpec(blk, lambda b, h: (v_head(h), z0 + b, 0)))
    args.append(v_arr)
    return pl.pallas_call(
        functools.partial(_ctx_kernel, two_part=len(q_parts) == 2, use_sink=sink is not None,
                          v_transposed=v_transposed),
        grid=(batch, n_heads),
        in_specs=in_specs,
        out_specs=pl.BlockSpec((ctx_len, LANE), lambda b, h: (b, h)),
        out_shape=jax.ShapeDtypeStruct((batch * ctx_len, n_heads * LANE), BF16),
        compiler_params=_cparams("parallel", "parallel"),
        name="ctx_attention",
    )(*args)


def _rope_angles(n_tokens, rot_dim):
    t = jnp.arange(n_tokens)
    row = (t // GRID_W).astype(F32)
    col = (t % GRID_W).astype(F32)
    n_freq = rot_dim // 4
    inv_freq = ROPE_BASE ** (-jnp.arange(n_freq, dtype=F32) / n_freq)
    ang = jnp.concatenate([row[:, None] * inv_freq, col[:, None] * inv_freq], axis=-1)
    return jnp.cos(ang), jnp.sin(ang)


def _with_identity_tile(tab, fill, tm):
    return jnp.concatenate([tab, jnp.broadcast_to(fill, (tm, LANE))], axis=0)


def swa_rope_tables(seq, scale, tm):
    cos, sin = _rope_angles(seq, HEAD_DIM)
    c = jnp.concatenate([cos, cos], axis=-1)
    s = jnp.concatenate([-sin, sin], axis=-1)
    one = jnp.ones((LANE,), F32)
    zero = jnp.zeros((LANE,), F32)
    ident_c = _with_identity_tile(jnp.ones_like(c), one, tm)
    ident_s = _with_identity_tile(jnp.zeros_like(s), zero, tm)
    c_sets = jnp.stack([_with_identity_tile(c * scale, one * scale, tm), _with_identity_tile(c, one, tm), ident_c])
    s_sets = jnp.stack([_with_identity_tile(s * scale, zero, tm), _with_identity_tile(s, zero, tm), ident_s])
    return c_sets, s_sets


def mla_rope_tables(seq, scale, tm):
    cos, sin = _rope_angles(seq, MLA_ROPE)
    half = MLA_ROPE // 2
    z = jnp.zeros((seq, half), F32)
    c = jnp.concatenate([cos, cos, z, z], axis=-1)
    s1 = jnp.concatenate([-sin, z, z, z], axis=-1)
    s2 = jnp.concatenate([z, sin, z, z], axis=-1)
    one = jnp.ones((LANE,), F32)
    zero = jnp.zeros((LANE,), F32)
    wt = _with_identity_tile
    c_sets = jnp.stack([wt(jnp.ones_like(c) * scale, one * scale, tm), wt(c * scale, one * scale, tm), wt(c, one, tm)])
    s1_sets = jnp.stack([wt(jnp.zeros_like(c), zero, tm), wt(s1 * scale, zero, tm), wt(s1, zero, tm)])
    s2_sets = jnp.stack([wt(jnp.zeros_like(c), zero, tm), wt(s2 * scale, zero, tm), wt(s2, zero, tm)])
    return c_sets, s1_sets, s2_sets


def kernel(x, c, ctx, c_ctx, w_mod, b_mod, norm_g, ffn_w_in, ffn_w_out, na_w_qkv, na_rpb, na_w_o,
           mla_w_down, mla_q_norm_g, mla_w_q_up, mla_kv_norm_g, mla_w_kv_up, mla_w_o,
           swa_w_qkv, swa_sink, swa_w_o, final_norm_g):
    batch, seq, d = x.shape
    ctx_len = ctx.shape[1]
    depth = w_mod.shape[0]
    tf_, tp, to_ = FFN_ROW_TILE, PROJ_ROW_TILE, OUT_ROW_TILE
    nx, nz = batch * seq, batch * ctx_len
    n_groups = batch + 1
    z0 = nx // ctx_len
    assert nx % ctx_len == 0 and all(seq % t == 0 and nz % t == 0 for t in (tf_, tp, to_))
    ffn_rows = dict(tiles_per_group=seq // tf_, n_groups=n_groups)
    proj_rows = dict(tiles_per_group=seq // tp, n_groups=n_groups, n_tiles=(nx + nz) // tp)
    pos_tiles = seq // tp

    n_cond = 8
    cond = jnp.concatenate([c, c_ctx[None, :], jnp.zeros((n_cond - batch - 1, d), F32)], axis=0)
    mods_all = mod_params(cond, w_mod, b_mod).reshape(depth, n_cond, N_MOD, d)

    w_in_all = ffn_w_in.astype(BF16)
    w_out_all = ffn_w_out.astype(BF16)
    qk_scale = HEAD_DIM ** -0.5 * LOG2E
    mla_scale = (MLA_NOPE + MLA_ROPE) ** -0.5 * LOG2E

    stream = None
    for li in range(depth):
        last = li == depth - 1
        j = li // N_MIXERS
        kind = li % N_MIXERS
        mods = mods_all[li]

        if li == 0:
            stream = ffn_half_step(x.reshape(nx, d), norm_g[li, 0], mods, w_in_all, w_out_all, (li, 0), k0=0,
                                   n_tiles=(nx + nz) // to_, x_tail=ctx.reshape(nz, d), tm=to_,
                                   tiles_per_group=seq // to_, n_groups=n_groups)
        else:
            stream = ffn_half_step(stream, norm_g[li, 0], mods, w_in_all, w_out_all, (li, 0), k0=0,
                                   n_tiles=(nx + nz) // tf_, **ffn_rows)

        proj = dict(mods=mods, mod_idx=(3, 4), **proj_rows)
        if kind == 0:
            col_scale = jnp.concatenate([jnp.full((NA_HEADS * HEAD_DIM,), qk_scale, F32),
                                         jnp.ones((2 * NA_HEADS * HEAD_DIM,), F32)])
            qkv = norm_linear(stream, 0, d, norm_g[li, 1], na_w_qkv[j].astype(BF16),
                              col_scale=col_scale, tn=1024, **proj)
            t2 = na_bias_table(na_rpb[j])
            ox = na_attention(qkv, t2, batch=batch, seq=seq, ctx_len=ctx_len)
            if not last:
                hd = lambda part: (qkv, lambda h: part * NA_HEADS + h)
                oz = ctx_attention([hd(0)], [hd(1)], hd(2), batch=batch, ctx_len=ctx_len,
                                   z0=z0, n_heads=NA_HEADS)
            w_o = na_w_o[j]
        elif kind == 1:
            w_down = mla_w_down[j]
            lora = MLA_Q_LORA + MLA_KV_LORA
            cqkv = norm_linear(stream, 0, d, norm_g[li, 1], w_down[:, :lora].astype(BF16),
                               layout="rows", out_dtype=F32, tn=lora, **proj)
            c_t, s1_t, s2_t = mla_rope_tables(seq, mla_scale, tp)
            w_kpe = jnp.pad(w_down[:, lora:], ((0, 0), (0, LANE - MLA_ROPE))).astype(BF16)
            kp = norm_linear(stream, 0, d, norm_g[li, 1], w_kpe, rope=(96, 32),
                             tables=(c_t[2:], s1_t[2:], s2_t[2:]), pos_tiles=pos_tiles,
                             tn=LANE, **proj)
            wq = mla_w_q_up[j].reshape(MLA_Q_LORA, MLA_HEADS, MLA_NOPE + MLA_ROPE)
            wq_n = wq[:, :, :MLA_NOPE].reshape(MLA_Q_LORA, MLA_HEADS * MLA_NOPE).astype(BF16)
            wq_p = jnp.pad(wq[:, :, MLA_NOPE:], ((0, 0), (0, 0), (0, LANE - MLA_ROPE)))
            wq_p = wq_p.reshape(MLA_Q_LORA, MLA_HEADS * LANE).astype(BF16)
            qn = norm_linear(cqkv, 0, MLA_Q_LORA, mla_q_norm_g[j], wq_n,
                             col_scale=jnp.full((MLA_HEADS * MLA_NOPE,), mla_scale, F32), tn=1024, **proj_rows)
            qp = norm_linear(cqkv, 0, MLA_Q_LORA, mla_q_norm_g[j], wq_p, rope=(96, 32),
                             tables=(c_t[1:2], s1_t[1:2], s2_t[1:2]), pos_tiles=pos_tiles, tn=512, **proj_rows)
            wkv = mla_w_kv_up[j].reshape(MLA_KV_LORA, MLA_HEADS, MLA_NOPE + MLA_V)
            w_kn = wkv[:, :, :MLA_NOPE].reshape(MLA_KV_LORA, -1).astype(BF16)
            w_v = wkv[:, :, MLA_NOPE:].reshape(MLA_KV_LORA, -1).astype(BF16)
            kn = norm_linear(cqkv, 1, MLA_KV_LORA, mla_kv_norm_g[j], w_kn, tn=1024, **proj_rows)
            vt = norm_linear(cqkv, 1, MLA_KV_LORA, mla_kv_norm_g[j], w_v, layout="heads_t", tn=1024, **proj_rows)
            ox = mla_attention(qn, qp, kn, kp, vt, batch=batch, seq=seq, ctx_len=ctx_len)
            if not last:
                oz = ctx_attention([(qn, lambda h: h), (qp, lambda h: h)],
                                   [(kn, lambda h: h), (kp, lambda h: 0)], (vt, lambda h: h),
                                   batch=batch, ctx_len=ctx_len, z0=z0, n_heads=MLA_HEADS, v_transposed=True)
            w_o = mla_w_o[j]
        else:
            c_t, s_t = swa_rope_tables(seq, qk_scale, tp)
            n_q_blocks = SWA_HEADS * HEAD_DIM // 512
            qkv = norm_linear(stream, 0, d, norm_g[li, 1], swa_w_qkv[j].astype(BF16), rope=(64,),
                              tables=(c_t, s_t), set_thresholds=(n_q_blocks, n_q_blocks + 1),
                              pos_tiles=pos_tiles, tn=512, **proj)
            ox = swa_attention(qkv, swa_sink[j], batch=batch, seq=seq, ctx_len=ctx_len)
            if not last:
                g = SWA_HEADS // SWA_KV_HEADS
                oz = ctx_attention([(qkv, lambda h: h)], [(qkv, lambda h: SWA_HEADS + h // g)],
                                   (qkv, lambda h: SWA_HEADS + SWA_KV_HEADS + h // g),
                                   batch=batch, ctx_len=ctx_len, z0=z0, n_heads=SWA_HEADS,
                                   sink=swa_sink[j])
            w_o = swa_w_o[j]

        n_rows = nx if last else nx + nz
        stream = linear_residual(ox, w_o.astype(BF16), stream, mods, k_gate=5, n_tiles=n_rows // to_,
                                 a_tail=None if last else oz, tiles_per_group=seq // to_, n_groups=n_groups)
        if last:
            stream = ffn_half_step(stream, norm_g[li, 2], mods, w_in_all, w_out_all, (li, 1), k0=6,
                                   n_tiles=n_rows // to_, final_gain=final_norm_g, tm=to_,
                                   tiles_per_group=seq // to_, n_groups=n_groups)
        else:
            stream = ffn_half_step(stream, norm_g[li, 2], mods, w_in_all, w_out_all, (li, 1), k0=6,
                                   n_tiles=n_rows // tf_, **ffn_rows)

    return stream.reshape(batch, seq, d)
```

```python
import functools
import math

import jax
import jax.numpy as jnp
from jax import lax
from jax.experimental import pallas as pl
from jax.experimental.pallas import tpu as pltpu

DEPTH = 4
GRID_W = 64
N_MIXERS = 3
N_MOD = 9
RMS_EPS = 1e-6
ROPE_BASE = 10000.0
NEG_INF = -1e30
LOG2E = math.log2(math.e)

NA_HEADS = 16
NA_KH = 8
NA_KW = 16
HEAD_DIM = 128

MLA_HEADS = 16
MLA_Q_LORA = 512
MLA_KV_LORA = 512
MLA_NOPE = 128
MLA_ROPE = 64
MLA_V = 128

SWA_HEADS = 16
SWA_KV_HEADS = 4
SWA_WINDOW = 128

LANE = 128
FFN_ROW_TILE = 1024
FFN_VMEM_LIMIT = 62 * 1024 * 1024
OUT_ROW_TILE = 512
PROJ_ROW_TILE = 1024
FFN_CHUNK = 512
FFN_CAST_CHUNK = 256
ROW_CHUNK = 128
VMEM_LIMIT = 56 * 1024 * 1024

F32 = jnp.float32
BF16 = jnp.bfloat16


def _cparams(*sem):
    return pltpu.CompilerParams(dimension_semantics=sem, vmem_limit_bytes=VMEM_LIMIT)


def _dot(a, b):
    return jnp.dot(a, b, preferred_element_type=F32)


def _dot_t(a, b):
    return lax.dot_general(a, b, (((1,), (1,)), ((), ())), preferred_element_type=F32)


def _sigmoid(x):
    return 1.0 / (1.0 + jnp.exp(-x))


def _rmsnorm_rows(x, g):
    ms = jnp.mean(x * x, axis=-1, keepdims=True)
    return x * lax.rsqrt(ms + RMS_EPS) * g


def _modulated_norm(read_x, g_ref, mod_ref, k_shift, k_scale):
    x = read_x()
    r = lax.rsqrt(jnp.mean(x * x, axis=-1, keepdims=True) + RMS_EPS)
    gs = g_ref[...] * (1.0 + mod_ref[0, k_scale:k_scale + 1, :])
    return (read_x() * r) * gs + mod_ref[0, k_shift:k_shift + 1, :]


def _group_fn(tiles_per_group, n_groups):
    return lambda i: jnp.minimum(i // tiles_per_group, n_groups - 1)


def _two_source_specs(block, split_tiles):
    return [pl.BlockSpec(block, lambda i, j: (jnp.minimum(i, split_tiles - 1), 0)),
            pl.BlockSpec(block, lambda i, j: (jnp.maximum(i - split_tiles, 0), 0))]


def _read_rows(ref, tail_ref, split_tiles, rows=slice(None), tile=None):
    if tail_ref is None:
        return ref[rows, :]
    tile = pl.program_id(0) if tile is None else tile
    return jnp.where(tile < split_tiles, ref[rows, :], tail_ref[rows, :])


def _for_row_chunks(n_rows, body):
    def step(c, carry):
        body(pl.ds(pl.multiple_of(c * ROW_CHUNK, ROW_CHUNK), ROW_CHUNK))
        return carry

    lax.fori_loop(0, n_rows // ROW_CHUNK, step, 0)


def _mod_kernel(c_ref, w_ref, b_ref, o_ref):
    c = c_ref[...]
    a = (c * _sigmoid(c)).astype(BF16)
    o_ref[0] = _dot(a, w_ref[0].astype(BF16)) + b_ref[0]


def mod_params(cond, w_mod, b_mod):
    depth, d, n = w_mod.shape
    g = cond.shape[0]
    tn = math.gcd(n, 1024)
    return pl.pallas_call(
        _mod_kernel,
        grid=(depth, n // tn),
        in_specs=[
            pl.BlockSpec((g, d), lambda l, j: (0, 0)),
            pl.BlockSpec((1, d, tn), lambda l, j: (l, 0, j)),
            pl.BlockSpec((1, 1, tn), lambda l, j: (l, 0, j)),
        ],
        out_specs=pl.BlockSpec((1, g, tn), lambda l, j: (l, 0, j)),
        out_shape=jax.ShapeDtypeStruct((depth, g, n), F32),
        compiler_params=_cparams("parallel", "parallel"),
        name="mod_params",
    )(cond, w_mod, b_mod.reshape(depth, 1, n))


def _ffn_kernel(*refs, k0, final_norm, split_tiles, tile_offset, cast_weights):
    it = iter(refs)
    x_ref = next(it)
    xt_ref = next(it) if split_tiles is not None else None
    g_ref, mod_ref, wg_ref, wu_ref, wo_ref = (next(it) for _ in range(5))
    gf_ref = next(it) if final_norm else None
    o_ref = next(it)
    w_bf_refs = [next(it) for _ in range(3)] if cast_weights else None
    h_ref = next(it)
    j = pl.program_id(1)
    tile = pl.program_id(0) + tile_offset

    @pl.when(j == 0)
    def _():
        def chunk(rows):
            read_x = lambda: _read_rows(x_ref, xt_ref, split_tiles, rows, tile)
            h_ref[rows, :] = _modulated_norm(read_x, g_ref, mod_ref, k0, k0 + 1).astype(BF16)
            o_ref[rows, :] = jnp.zeros((ROW_CHUNK, o_ref.shape[1]), F32)

        _for_row_chunks(o_ref.shape[0], chunk)

    wg, wu, wo = wg_ref[...], wu_ref[...], wo_ref[...]
    if cast_weights:
        wg, wu, wo = wg.astype(BF16), wu.astype(BF16), wo.astype(BF16)
        for ref, w in zip(w_bf_refs, (wg, wu, wo)):
            ref[...] = w
    h = h_ref[...]
    gate = _dot(h, wg)
    up = _dot(h, wu)
    a = (gate * _sigmoid(gate) * up).astype(BF16)
    o_ref[...] += _dot(a, wo)

    @pl.when(j == pl.num_programs(1) - 1)
    def _():
        def chunk(rows):
            x = _read_rows(x_ref, xt_ref, split_tiles, rows, tile)
            y = x + (0.5 * mod_ref[0, k0 + 2:k0 + 3, :]) * o_ref[rows, :]
            if final_norm:
                y = _rmsnorm_rows(y, gf_ref[...])
            o_ref[rows, :] = y

        _for_row_chunks(o_ref.shape[0], chunk)


def stacked_ffn_weights(w_in, w_out, lead, tf):
    d, f = w_in.shape[-2], w_out.shape[-2]
    nf = f // tf
    sq = (None,) * len(lead)
    return ((w_in, sq + (d, tf), lambda i, j: lead + (0, j)),
            (w_in, sq + (d, tf), lambda i, j: lead + (0, j + nf)),
            (w_out, sq + (tf, d), lambda i, j: lead + (j, 0)))


def plain_ffn_weights(w_gate, w_up, w_down, tf):
    d = w_gate.shape[0]
    return ((w_gate, (d, tf), lambda i, j: (0, j)),
            (w_up, (d, tf), lambda i, j: (0, j)),
            (w_down, (tf, d), lambda i, j: (j, 0)))


def ffn_half_step(x, gain, mods, weights, *, k0, n_tiles, total_tiles, tile_offset, tiles_per_group,
                  n_groups, tf, x_tail=None, final_gain=None, cast_weights=False, in_place=False,
                  tm=FFN_ROW_TILE):
    d = x.shape[1]
    f = weights[2][0].shape[-2]
    nf = f // tf
    grp = _group_fn(tiles_per_group, n_groups)
    split_tiles = None if x_tail is None else x.shape[0] // tm
    one_tile = n_tiles == 1
    if x_tail is None:
        in_specs = [pl.BlockSpec((tm, d), lambda i, j: (i + tile_offset, 0),
                                 **(dict(pipeline_mode=pl.Buffered(1)) if one_tile else {}))]
        args = [x]
    else:
        in_specs = [pl.BlockSpec((tm, d), lambda i, j: (jnp.minimum(i + tile_offset, split_tiles - 1), 0),
                                 pipeline_mode=pl.Buffered(1)),
                    pl.BlockSpec((tm, d), lambda i, j: (jnp.maximum(i + tile_offset - split_tiles, 0), 0),
                                 pipeline_mode=pl.Buffered(1))]
        args = [x, x_tail]
    in_specs += [pl.BlockSpec((1, d), lambda i, j: (0, 0)),
                 pl.BlockSpec((1, N_MOD, d), lambda i, j: (grp(i + tile_offset), 0, 0))]
    args += [gain.reshape(1, d), mods]
    for arr, block, index_map in weights:
        in_specs.append(pl.BlockSpec(block, index_map))
        args.append(arr)
    if final_gain is not None:
        in_specs.append(pl.BlockSpec((1, d), lambda i, j: (0, 0)))
        args.append(final_gain.reshape(1, d))
    if in_place:
        assert x_tail is None and x.shape[0] == total_tiles * tm
    out_specs = [pl.BlockSpec((tm, d), lambda i, j: (i + tile_offset, 0))]
    out_shape = [jax.ShapeDtypeStruct((total_tiles * tm, d), F32)]
    if cast_weights:
        out_specs += [pl.BlockSpec((d, tf), lambda i, j: (0, j)), pl.BlockSpec((d, tf), lambda i, j: (0, j)),
                      pl.BlockSpec((tf, d), lambda i, j: (j, 0))]
        out_shape += [jax.ShapeDtypeStruct((d, f), BF16), jax.ShapeDtypeStruct((d, f), BF16),
                      jax.ShapeDtypeStruct((f, d), BF16)]
    out = pl.pallas_call(
        functools.partial(_ffn_kernel, k0=k0, final_norm=final_gain is not None, split_tiles=split_tiles,
                          tile_offset=tile_offset, cast_weights=cast_weights),
        grid=(n_tiles, nf),
        in_specs=in_specs,
        out_specs=out_specs,
        out_shape=out_shape,
        input_output_aliases={0: 0} if in_place else {},
        scratch_shapes=[pltpu.VMEM((tm, d), BF16)],
        compiler_params=pltpu.CompilerParams(dimension_semantics=("parallel", "arbitrary"),
                                             vmem_limit_bytes=FFN_VMEM_LIMIT),
        name="ffn_half_step",
    )(*args)
    return out if cast_weights else out[0]


def ffn_half_step_f32_weights(x, gain, mods, w_in, w_out, lead, *, total_tiles, **kw):
    y, w_gate, w_up, w_down = ffn_half_step(
        x, gain, mods, stacked_ffn_weights(w_in, w_out, lead, FFN_CAST_CHUNK), n_tiles=1, total_tiles=total_tiles,
        tile_offset=0, tf=FFN_CAST_CHUNK, cast_weights=True, in_place=True, **kw)
    return ffn_half_step(
        y, gain, mods, plain_ffn_weights(w_gate, w_up, w_down, FFN_CHUNK), n_tiles=total_tiles - 1,
        total_tiles=total_tiles, tile_offset=1, tf=FFN_CHUNK, in_place=True, **kw)


def _norm_linear_kernel(*refs, mod_idx, rope, layout, n_chunks):
    it = iter(refs)
    x_ref, g_ref = next(it), next(it)
    mod_ref = next(it) if mod_idx is not None else None
    w_ref = next(it)
    if rope is None:
        cs_ref = next(it)
    else:
        tab_refs = [next(it) for _ in range(len(rope) + 1)]
    o_ref, h_ref = next(it), next(it)

    @pl.when(pl.program_id(1) == 0)
    def _():
        if mod_idx is None:
            y = _rmsnorm_rows(x_ref[...], g_ref[...])
        else:
            y = _modulated_norm(lambda: x_ref[...], g_ref, mod_ref, *mod_idx)
        h_ref[...] = y.astype(BF16)

    acc = _dot(h_ref[...], w_ref[...])
    if rope is None:
        acc = acc * cs_ref[...]
    for c in range(n_chunks):
        a = acc[:, c * LANE:(c + 1) * LANE]
        if rope is not None:
            out = a * tab_refs[0][0]
            for shift, tab in zip(rope, tab_refs[1:]):
                out = out + pltpu.roll(a, shift, 1) * tab[0]
            a = out
        if layout == "heads":
            o_ref[c] = a.astype(o_ref.dtype)
        elif layout == "heads_t":
            o_ref[c] = a.T.astype(o_ref.dtype)
        else:
            o_ref[:, c * LANE:(c + 1) * LANE] = a.astype(o_ref.dtype)


def norm_linear(x, x_col_block, k, gain, w, *, n_tiles, tiles_per_group, n_groups,
                mods=None, mod_idx=None, col_scale=None, rope=None, tables=None,
                set_thresholds=(), pos_tiles=None, layout="heads", out_dtype=BF16,
                tm=PROJ_ROW_TILE, tn=512):
    n = w.shape[1]
    nj = n // tn
    n_chunks = tn // LANE
    grp = _group_fn(tiles_per_group, n_groups)
    in_specs = [
        pl.BlockSpec((tm, k), lambda i, j: (i, x_col_block)),
        pl.BlockSpec((1, k), lambda i, j: (0, 0)),
    ]
    args = [x, gain.reshape(1, k)]
    if mod_idx is not None:
        in_specs.append(pl.BlockSpec((1, N_MOD, k), lambda i, j: (grp(i), 0, 0)))
        args.append(mods)
    in_specs.append(pl.BlockSpec((k, tn), lambda i, j: (0, j)))
    args.append(w)
    if rope is None:
        if col_scale is None:
            col_scale = jnp.ones((n,), F32)
        in_specs.append(pl.BlockSpec((1, tn), lambda i, j: (0, j)))
        args.append(col_scale.reshape(1, n))
    else:
        n_lat_tiles = tiles_per_group * (n_groups - 1)

        def tab_map(i, j):
            s = sum((j >= t).astype(jnp.int32) for t in set_thresholds) if set_thresholds else 0
            p = jnp.where(i < n_lat_tiles, i % pos_tiles, pos_tiles)
            return (s, p, 0)

        for t in tables:
            in_specs.append(pl.BlockSpec((1, tm, LANE), tab_map))
            args.append(t)
    rows = n_tiles * tm
    if layout == "heads":
        out_spec = pl.BlockSpec((n_chunks, tm, LANE), lambda i, j: (j, i, 0))
        out_shape = jax.ShapeDtypeStruct((n // LANE, rows, LANE), out_dtype)
    elif layout == "heads_t":
        out_spec = pl.BlockSpec((n_chunks, LANE, tm), lambda i, j: (j, 0, i))
        out_shape = jax.ShapeDtypeStruct((n // LANE, LANE, rows), out_dtype)
    else:
        out_spec = pl.BlockSpec((tm, tn), lambda i, j: (i, j))
        out_shape = jax.ShapeDtypeStruct((rows, n), out_dtype)
    return pl.pallas_call(
        functools.partial(_norm_linear_kernel, mod_idx=mod_idx, rope=rope,
                          layout=layout, n_chunks=n_chunks),
        grid=(n_tiles, nj),
        in_specs=in_specs,
        out_specs=out_spec,
        out_shape=out_shape,
        scratch_shapes=[pltpu.VMEM((tm, k), BF16)],
        compiler_params=_cparams("parallel", "arbitrary"),
        name="norm_linear",
    )(*args)


def _linear_residual_kernel(*refs, k_gate, split_tiles):
    it = iter(refs)
    a_ref = next(it)
    at_ref = next(it) if split_tiles is not None else None
    w_ref, x_ref, mod_ref, o_ref = next(it), next(it), next(it), next(it)
    acc = _dot(_read_rows(a_ref, at_ref, split_tiles), w_ref[...])
    o_ref[...] = x_ref[...] + mod_ref[0, k_gate:k_gate + 1, :] * acc


def linear_residual(a, w, x, mods, *, k_gate, n_tiles, tiles_per_group, n_groups, a_tail=None,
                    tm=OUT_ROW_TILE):
    kdim, n = w.shape
    grp = _group_fn(tiles_per_group, n_groups)
    split_tiles = None if a_tail is None else a.shape[0] // tm
    if a_tail is None:
        in_specs, args = [pl.BlockSpec((tm, kdim), lambda i, j: (i, 0))], [a]
    else:
        in_specs, args = _two_source_specs((tm, kdim), split_tiles), [a, a_tail]
    in_specs += [
        pl.BlockSpec((kdim, n), lambda i, j: (0, 0)),
        pl.BlockSpec((tm, n), lambda i, j: (i, 0)),
        pl.BlockSpec((1, N_MOD, n), lambda i, j: (grp(i), 0, 0)),
    ]
    args += [w, x, mods]
    return pl.pallas_call(
        functools.partial(_linear_residual_kernel, k_gate=k_gate, split_tiles=split_tiles),
        grid=(n_tiles, 1),
        in_specs=in_specs,
        out_specs=pl.BlockSpec((tm, n), lambda i, j: (i, 0)),
        out_shape=jax.ShapeDtypeStruct((n_tiles * tm, n), F32),
        compiler_params=_cparams("parallel", "arbitrary"),
        name="linear_residual",
    )(*args)


NA_QROWS = 4
NA_QTOK = NA_QROWS * GRID_W
NA_BAND_BLOCKS = 3
NA_PAIRS = 2 * NA_KH


def _na_bias_kernel(rpb_ref, o_ref):
    h = pl.program_id(0)
    cq = lax.broadcasted_iota(jnp.int32, (GRID_W, 2 * GRID_W), 0)
    lane = lax.broadcasted_iota(jnp.int32, (GRID_W, 2 * GRID_W), 1)
    second = lane >= GRID_W
    ck = jnp.where(second, lane - GRID_W, lane)
    n_dc = 2 * NA_KW - 1
    code = ck - cq + (NA_KW - 1) + jnp.where(second, n_dc, 0)
    start = jnp.clip(cq - NA_KW // 2, 0, GRID_W - NA_KW)
    in_win = (ck >= start) & (ck < start + NA_KW)
    for p in range(NA_PAIRS):
        tile = jnp.zeros((GRID_W, 2 * GRID_W), F32)
        for half in range(2):
            dr = p - NA_KH + half
            if not (-(NA_KH - 1) <= dr <= NA_KH - 1):
                continue
            for dc in range(n_dc):
                val = rpb_ref[h, dr + NA_KH - 1, dc]
                tile = jnp.where(code == half * n_dc + dc, val, tile)
        o_ref[0, p] = jnp.where(in_win, tile * LOG2E, NEG_INF)


def na_bias_table(rpb):
    heads = rpb.shape[0]
    return pl.pallas_call(
        _na_bias_kernel,
        grid=(heads,),
        in_specs=[pl.BlockSpec(memory_space=pltpu.SMEM)],
        out_specs=pl.BlockSpec((1, NA_PAIRS, GRID_W, 2 * GRID_W), lambda h: (h, 0, 0, 0)),
        out_shape=jax.ShapeDtypeStruct((heads, NA_PAIRS, GRID_W, 2 * GRID_W), F32),
        compiler_params=_cparams("parallel"),
        name="na_bias_table",
    )(rpb)


def _na_kernel(q_ref, k0_ref, k1_ref, k2_ref, v0_ref, v1_ref, v2_ref, kz_ref, vz_ref,
               t2_ref, o_ref, vm_ref, ob_ref, *, rows):
    qb = pl.program_id(1)
    r0 = NA_QROWS * qb
    u0 = jnp.clip(r0 - NA_KH // 2, 0, rows - NA_BAND_BLOCKS * NA_QROWS)
    k_refs = (k0_ref, k1_ref, k2_ref, kz_ref)
    v_refs = (v0_ref, v1_ref, v2_ref, vz_ref)

    q_row = r0 + (lax.broadcasted_iota(jnp.int32, (NA_QTOK, NA_QTOK), 0) >> 6)
    k_off = lax.broadcasted_iota(jnp.int32, (NA_QTOK, NA_QTOK), 1) >> 6
    rs = jnp.clip(q_row - NA_KH // 2, 0, rows - NA_KH)
    for i in range(NA_BAND_BLOCKS):
        rk = u0 + NA_QROWS * i + k_off
        vm_ref[i] = jnp.where((rk >= rs) & (rk < rs + NA_KH), 0.0, NEG_INF)

    def scores(h):
        return _dot_t(q_ref[h], jnp.concatenate([r[h] for r in k_refs], axis=0))

    def finish_head(h, s_all):
        s = []
        for i in range(NA_BAND_BLOCKS):
            bias_rows = []
            for qi in range(NA_QROWS):
                pair = []
                for c in range(2):
                    dr0 = u0 + NA_QROWS * i + 2 * c - (r0 + qi)
                    pair.append(t2_ref[h, jnp.clip(dr0 + NA_KH, 0, NA_PAIRS - 1)])
                bias_rows.append(jnp.concatenate(pair, axis=1))
            s.append(s_all[:, i * NA_QTOK:(i + 1) * NA_QTOK] + jnp.concatenate(bias_rows, axis=0) + vm_ref[i])
        s.append(s_all[:, NA_BAND_BLOCKS * NA_QTOK:])
        m = jnp.maximum(jnp.maximum(s[0], s[1]), jnp.maximum(s[2], s[3])).max(axis=-1, keepdims=True)
        p = jnp.concatenate([jnp.exp2(si - m) for si in s], axis=1).astype(BF16)
        v = jnp.concatenate([r[h] for r in v_refs], axis=0)
        o = _dot(p, jnp.concatenate([v, jnp.ones_like(v)], axis=1))
        ob_ref[h] = (o[:, :HEAD_DIM] * (1.0 / o[:, HEAD_DIM:])).astype(BF16)

    s_next = scores(0)
    for h in range(NA_HEADS):
        s_cur = s_next
        if h + 1 < NA_HEADS:
            s_next = scores(h + 1)
        finish_head(h, s_cur)
    for h in range(NA_HEADS):
        o_ref[:, h * HEAD_DIM:(h + 1) * HEAD_DIM] = ob_ref[h]


def na_attention(qkv, t2, *, batch, seq, ctx_len):
    rows = seq // GRID_W
    nqb = seq // NA_QTOK
    z0 = batch * seq // ctx_len
    assert ctx_len == NA_QTOK and rows >= NA_BAND_BLOCKS * NA_QROWS
    blk = (NA_HEADS, NA_QTOK, HEAD_DIM)

    def band(i):
        return lambda b, q: (0, b * nqb + jnp.clip(q - 1, 0, nqb - NA_BAND_BLOCKS) + i, 0)

    def shifted(fn, part):
        return lambda b, q: (part,) + fn(b, q)[1:]

    in_specs = [pl.BlockSpec(blk, lambda b, q: (0, b * nqb + q, 0))]
    in_specs += [pl.BlockSpec(blk, shifted(band(i), 1)) for i in range(NA_BAND_BLOCKS)]
    in_specs += [pl.BlockSpec(blk, shifted(band(i), 2)) for i in range(NA_BAND_BLOCKS)]
    in_specs += [pl.BlockSpec(blk, lambda b, q: (1, z0 + b, 0)),
                 pl.BlockSpec(blk, lambda b, q: (2, z0 + b, 0)),
                 pl.BlockSpec(t2.shape, lambda b, q: (0, 0, 0, 0))]
    return pl.pallas_call(
        functools.partial(_na_kernel, rows=rows),
        grid=(batch, nqb),
        in_specs=in_specs,
        out_specs=pl.BlockSpec((NA_QTOK, NA_HEADS * HEAD_DIM), lambda b, q: (b * nqb + q, 0)),
        out_shape=jax.ShapeDtypeStruct((batch * seq, NA_HEADS * HEAD_DIM), BF16),
        scratch_shapes=[pltpu.VMEM((NA_BAND_BLOCKS, NA_QTOK, NA_QTOK), F32),
                        pltpu.VMEM(blk, BF16)],
        compiler_params=_cparams("parallel", "arbitrary"),
        name="na_attention",
    )(*([qkv] * 9), t2)


MLA_TQ = 1024
MLA_TK = 1024


def _mla_kernel(qn_ref, qp_ref, kn_ref, kp_ref, vt_ref, knz_ref, kpz_ref, vtz_ref, o_ref, *, seq):
    q = jnp.concatenate([qn_ref[0], qp_ref[0]], axis=1)
    tq = q.shape[0]

    n_blocks = seq // MLA_TK

    def scores(kb):
        if kb == n_blocks:
            return _dot_t(jnp.concatenate([knz_ref[0], kpz_ref[0]], axis=1), q)
        rows = slice(kb * MLA_TK, (kb + 1) * MLA_TK)
        return _dot_t(jnp.concatenate([kn_ref[0, rows, :], kp_ref[0, rows, :]], axis=1), q)

    m = jnp.full((1, tq), NEG_INF, F32)
    l = jnp.zeros((1, tq), F32)
    acc = jnp.zeros((MLA_V, tq), F32)
    s_next = scores(0)
    for kb in range(n_blocks + 1):
        s = s_next
        if kb < n_blocks:
            s_next = scores(kb + 1)
            vt = vt_ref[0, :, kb * MLA_TK:(kb + 1) * MLA_TK]
        else:
            vt = vtz_ref[0]
        m_new = jnp.maximum(m, s.max(axis=0, keepdims=True))
        alpha = jnp.exp2(m - m_new)
        p = jnp.exp2(s - m_new)
        l = alpha * l + p.sum(axis=0, keepdims=True)
        acc = alpha * acc + _dot(vt, p.astype(BF16))
        m = m_new
    o_ref[...] = (acc * (1.0 / l)).T.astype(BF16)


def mla_attention(qn, qp, kn, kp, vt, *, batch, seq, ctx_len):
    h_ = MLA_HEADS
    nqb = seq // MLA_TQ
    z0 = batch * seq // ctx_len
    qblk = (1, MLA_TQ, LANE)
    kblk = (1, seq, LANE)
    zblk = (1, ctx_len, LANE)
    in_specs = [
        pl.BlockSpec(qblk, lambda b, h, q: (h, b * nqb + q, 0)),
        pl.BlockSpec(qblk, lambda b, h, q: (h, b * nqb + q, 0)),
        pl.BlockSpec(kblk, lambda b, h, q: (h, b, 0)),
        pl.BlockSpec(kblk, lambda b, h, q: (0, b, 0)),
        pl.BlockSpec((1, MLA_V, seq), lambda b, h, q: (h, 0, b)),
        pl.BlockSpec(zblk, lambda b, h, q: (h, z0 + b, 0)),
        pl.BlockSpec(zblk, lambda b, h, q: (0, z0 + b, 0)),
        pl.BlockSpec((1, MLA_V, ctx_len), lambda b, h, q: (h, 0, z0 + b)),
    ]
    return pl.pallas_call(
        functools.partial(_mla_kernel, seq=seq),
        grid=(batch, h_, nqb),
        in_specs=in_specs,
        out_specs=pl.BlockSpec((MLA_TQ, MLA_V), lambda b, h, q: (b * nqb + q, h)),
        out_shape=jax.ShapeDtypeStruct((batch * seq, h_ * MLA_V), BF16),
        compiler_params=_cparams("parallel", "parallel", "arbitrary"),
        name="mla_attention",
    )(qn, qp, kn, kp, vt, kn, kp, vt)


SWA_TQ = SWA_WINDOW


def _swa_kernel(sink_ref, q_ref, k0_ref, k1_ref, k2_ref, v0_ref, v1_ref, v2_ref,
                kz_ref, vz_ref, o_ref, *, n_blocks):
    qb = pl.program_id(1)
    g = SWA_HEADS // SWA_KV_HEADS
    m_rows = g * SWA_TQ
    row = lax.broadcasted_iota(jnp.int32, (m_rows, SWA_TQ), 0) & (SWA_TQ - 1)
    lane = lax.broadcasted_iota(jnp.int32, (m_rows, SWA_TQ), 1)
    mask_prev = (lane >= row) & (qb >= 1)
    mask_next = (lane <= row) & (qb <= n_blocks - 2)
    grp_id = lax.broadcasted_iota(jnp.int32, (m_rows, 1), 0) >> 7
    k_refs = (k0_ref, k1_ref, k2_ref, kz_ref)
    v_refs = (v0_ref, v1_ref, v2_ref, vz_ref)

    def scores(kvh):
        q4 = q_ref[kvh * g:(kvh + 1) * g].reshape(m_rows, HEAD_DIM)
        return _dot_t(q4, jnp.concatenate([r[kvh] for r in k_refs], axis=0))

    s_next = scores(0)
    for kvh in range(SWA_KV_HEADS):
        s_all = s_next
        if kvh + 1 < SWA_KV_HEADS:
            s_next = scores(kvh + 1)
        sink = jnp.zeros((m_rows, 1), F32)
        for gi in range(g):
            sink = jnp.where(grp_id == gi, sink_ref[kvh * g + gi] * LOG2E, sink)
        s = [jnp.where(mask_prev, s_all[:, :SWA_TQ], NEG_INF),
             s_all[:, SWA_TQ:2 * SWA_TQ],
             jnp.where(mask_next, s_all[:, 2 * SWA_TQ:3 * SWA_TQ], NEG_INF)]
        s += [s_all[:, c:c + SWA_TQ] for c in range(3 * SWA_TQ, s_all.shape[1], SWA_TQ)]
        mx = s[0]
        for si in s[1:]:
            mx = jnp.maximum(mx, si)
        m = jnp.maximum(mx.max(axis=-1, keepdims=True), sink)
        p = [jnp.exp2(si - m) for si in s]
        ps = p[0]
        for pi in p[1:]:
            ps = ps + pi
        l = ps.sum(axis=-1, keepdims=True) + jnp.exp2(sink - m)
        o = _dot(jnp.concatenate(p, axis=1).astype(BF16), jnp.concatenate([r[kvh] for r in v_refs], axis=0))
        o = (o * (1.0 / l)).astype(BF16)
        for gi in range(g):
            hq = kvh * g + gi
            o_ref[:, hq * HEAD_DIM:(hq + 1) * HEAD_DIM] = o[gi * SWA_TQ:(gi + 1) * SWA_TQ]


def swa_attention(qkv, sink, *, batch, seq, ctx_len):
    nb = seq // SWA_TQ
    kvh = SWA_KV_HEADS
    qpart = SWA_HEADS // kvh
    z0 = batch * seq // ctx_len
    kblk = (kvh, SWA_TQ, HEAD_DIM)
    zblk = (kvh, ctx_len, HEAD_DIM)
    assert ctx_len % SWA_TQ == 0

    def kmap(part, d):
        return lambda b, q: (part, b * nb + jnp.clip(q + d, 0, nb - 1), 0)

    in_specs = [pl.BlockSpec(memory_space=pltpu.SMEM),
                pl.BlockSpec((SWA_HEADS, SWA_TQ, HEAD_DIM), lambda b, q: (0, b * nb + q, 0))]
    in_specs += [pl.BlockSpec(kblk, kmap(qpart, d)) for d in (-1, 0, 1)]
    in_specs += [pl.BlockSpec(kblk, kmap(qpart + 1, d)) for d in (-1, 0, 1)]
    in_specs += [pl.BlockSpec(zblk, lambda b, q: (qpart, z0 + b, 0)),
                 pl.BlockSpec(zblk, lambda b, q: (qpart + 1, z0 + b, 0))]
    return pl.pallas_call(
        functools.partial(_swa_kernel, n_blocks=nb),
        grid=(batch, nb),
        in_specs=in_specs,
        out_specs=pl.BlockSpec((SWA_TQ, SWA_HEADS * HEAD_DIM), lambda b, q: (b * nb + q, 0)),
        out_shape=jax.ShapeDtypeStruct((batch * seq, SWA_HEADS * HEAD_DIM), BF16),
        compiler_params=_cparams("parallel", "arbitrary"),
        name="swa_attention",
    )(sink, *([qkv] * 9))


def _ctx_kernel(*refs, two_part, use_sink, v_transposed):
    it = iter(refs)
    sink_ref = next(it) if use_sink else None
    if two_part:
        q = jnp.concatenate([next(it)[0], next(it)[0]], axis=1)
        k = jnp.concatenate([next(it)[0], next(it)[0]], axis=1)
    else:
        q = next(it)[0]
        k = next(it)[0]
    v_ref, o_ref = next(it), next(it)
    s = _dot_t(q, k)
    m = s.max(axis=-1, keepdims=True)
    if use_sink:
        sink = sink_ref[pl.program_id(1)] * LOG2E
        m = jnp.maximum(m, sink)
    p = jnp.exp2(s - m)
    l = p.sum(axis=-1, keepdims=True)
    if use_sink:
        l = l + jnp.exp2(sink - m)
    pv = _dot_t(p.astype(BF16), v_ref[0]) if v_transposed else _dot(p.astype(BF16), v_ref[0])
    o_ref[...] = (pv * (1.0 / l)).astype(BF16)


def ctx_attention(q_parts, k_parts, v_part, *, batch, ctx_len, z0, n_heads, sink=None,
                  v_transposed=False):
    blk = (1, ctx_len, LANE)
    in_specs, args = [], []
    if sink is not None:
        in_specs.append(pl.BlockSpec(memory_space=pltpu.SMEM))
        args.append(sink)
    for arr, head_fn in (*q_parts, *k_parts):
        in_specs.append(pl.BlockSpec(blk, functools.partial(
            lambda b, h, head_fn: (head_fn(h), z0 + b, 0), head_fn=head_fn)))
        args.append(arr)
    v_arr, v_head = v_part
    if v_transposed:
        in_specs.append(pl.BlockSpec((1, LANE, ctx_len), lambda b, h: (v_head(h), 0, z0 + b)))
    else:
        in_specs.append(pl.BlockSpec(blk, lambda b, h: (v_head(h), z0 + b, 0)))
    args.append(v_arr)
    return pl.pallas_call(
        functools.partial(_ctx_kernel, two_part=len(q_parts) == 2, use_sink=sink is not None,
                          v_transposed=v_transposed),
        grid=(batch, n_heads),
        in_specs=in_specs,
        out_specs=pl.BlockSpec((ctx_len, LANE), lambda b, h: (b, h)),
        out_shape=jax.ShapeDtypeStruct((batch * ctx_len, n_heads * LANE), BF16),
        compiler_params=_cparams("parallel", "parallel"),
        name="ctx_attention",
    )(*args)


def _rope_angles(n_tokens, rot_dim):
    t = jnp.arange(n_tokens)
    row = (t // GRID_W).astype(F32)
    col = (t % GRID_W).astype(F32)
    n_freq = rot_dim // 4
    inv_freq = ROPE_BASE ** (-jnp.arange(n_freq, dtype=F32) / n_freq)
    ang = jnp.concatenate([row[:, None] * inv_freq, col[:, None] * inv_freq], axis=-1)
    return jnp.cos(ang), jnp.sin(ang)


def _with_identity_tile(tab, fill, tm):
    return jnp.concatenate([tab, jnp.broadcast_to(fill, (tm, LANE))], axis=0)


def swa_rope_tables(seq, scale, tm):
    cos, sin = _rope_angles(seq, HEAD_DIM)
    c = jnp.concatenate([cos, cos], axis=-1)
    s = jnp.concatenate([-sin, sin], axis=-1)
    one = jnp.ones((LANE,), F32)
    zero = jnp.zeros((LANE,), F32)
    ident_c = _with_identity_tile(jnp.ones_like(c), one, tm)
    ident_s = _with_identity_tile(jnp.zeros_like(s), zero, tm)
    c_sets = jnp.stack([_with_identity_tile(c * scale, one * scale, tm), _with_identity_tile(c, one, tm), ident_c])
    s_sets = jnp.stack([_with_identity_tile(s * scale, zero, tm), _with_identity_tile(s, zero, tm), ident_s])
    return c_sets, s_sets


def mla_rope_tables(seq, scale, tm):
    cos, sin = _rope_angles(seq, MLA_ROPE)
    half = MLA_ROPE // 2
    z = jnp.zeros((seq, half), F32)
    c = jnp.concatenate([cos, cos, z, z], axis=-1)
    s1 = jnp.concatenate([-sin, z, z, z], axis=-1)
    s2 = jnp.concatenate([z, sin, z, z], axis=-1)
    one = jnp.ones((LANE,), F32)
    zero = jnp.zeros((LANE,), F32)
    wt = _with_identity_tile
    c_sets = jnp.stack([wt(jnp.ones_like(c) * scale, one * scale, tm), wt(c * scale, one * scale, tm), wt(c, one, tm)])
    s1_sets = jnp.stack([wt(jnp.zeros_like(c), zero, tm), wt(s1 * scale, zero, tm), wt(s1, zero, tm)])
    s2_sets = jnp.stack([wt(jnp.zeros_like(c), zero, tm), wt(s2 * scale, zero, tm), wt(s2, zero, tm)])
    return c_sets, s1_sets, s2_sets


def kernel(x, c, ctx, c_ctx, w_mod, b_mod, norm_g, ffn_w_in, ffn_w_out, na_w_qkv, na_rpb, na_w_o,
           mla_w_down, mla_q_norm_g, mla_w_q_up, mla_kv_norm_g, mla_w_kv_up, mla_w_o,
           swa_w_qkv, swa_sink, swa_w_o, final_norm_g):
    batch, seq, d = x.shape
    ctx_len = ctx.shape[1]
    depth = w_mod.shape[0]
    tf_, tp, to_ = FFN_ROW_TILE, PROJ_ROW_TILE, OUT_ROW_TILE
    nx, nz = batch * seq, batch * ctx_len
    n_groups = batch + 1
    z0 = nx // ctx_len
    assert nx % ctx_len == 0 and all(seq % t == 0 and nz % t == 0 for t in (tf_, tp, to_))
    ffn_rows = dict(tiles_per_group=seq // tf_, n_groups=n_groups)
    proj_rows = dict(tiles_per_group=seq // tp, n_groups=n_groups, n_tiles=(nx + nz) // tp)
    pos_tiles = seq // tp

    n_cond = 8
    cond = jnp.concatenate([c, c_ctx[None, :], jnp.zeros((n_cond - batch - 1, d), F32)], axis=0)
    mods_all = mod_params(cond, w_mod, b_mod).reshape(depth, n_cond, N_MOD, d)

    qk_scale = HEAD_DIM ** -0.5 * LOG2E
    mla_scale = (MLA_NOPE + MLA_ROPE) ** -0.5 * LOG2E

    stream = None
    for li in range(depth):
        last = li == depth - 1
        j = li // N_MIXERS
        kind = li % N_MIXERS
        mods = mods_all[li]

        if li == 0:
            first_w = stacked_ffn_weights(ffn_w_in[0, 0].astype(BF16), ffn_w_out[0, 0].astype(BF16), (), FFN_CHUNK)
            all_tiles = (nx + nz) // tf_
            stream = ffn_half_step(x.reshape(nx, d), norm_g[li, 0], mods, first_w, k0=0, n_tiles=all_tiles,
                                   total_tiles=all_tiles, tile_offset=0, tf=FFN_CHUNK,
                                   x_tail=ctx.reshape(nz, d), **ffn_rows)
        else:
            stream = ffn_half_step_f32_weights(stream, norm_g[li, 0], mods, ffn_w_in, ffn_w_out, (li, 0),
                                               k0=0, total_tiles=(nx + nz) // tf_, **ffn_rows)

        proj = dict(mods=mods, mod_idx=(3, 4), **proj_rows)
        if kind == 0:
            col_scale = jnp.concatenate([jnp.full((NA_HEADS * HEAD_DIM,), qk_scale, F32),
                                         jnp.ones((2 * NA_HEADS * HEAD_DIM,), F32)])
            qkv = norm_linear(stream, 0, d, norm_g[li, 1], na_w_qkv[j].astype(BF16),
                              col_scale=col_scale, tn=1024, **proj)
            t2 = na_bias_table(na_rpb[j])
            ox = na_attention(qkv, t2, batch=batch, seq=seq, ctx_len=ctx_len)
            if not last:
                hd = lambda part: (qkv, lambda h: part * NA_HEADS + h)
                oz = ctx_attention([hd(0)], [hd(1)], hd(2), batch=batch, ctx_len=ctx_len,
                                   z0=z0, n_heads=NA_HEADS)
            w_o = na_w_o[j]
        elif kind == 1:
            w_down = mla_w_down[j]
            lora = MLA_Q_LORA + MLA_KV_LORA
            cqkv = norm_linear(stream, 0, d, norm_g[li, 1], w_down[:, :lora].astype(BF16),
                               layout="rows", out_dtype=F32, tn=lora, **proj)
            c_t, s1_t, s2_t = mla_rope_tables(seq, mla_scale, tp)
            w_kpe = jnp.pad(w_down[:, lora:], ((0, 0), (0, LANE - MLA_ROPE))).astype(BF16)
            kp = norm_linear(stream, 0, d, norm_g[li, 1], w_kpe, rope=(96, 32),
                             tables=(c_t[2:], s1_t[2:], s2_t[2:]), pos_tiles=pos_tiles,
                             tn=LANE, **proj)
            wq = mla_w_q_up[j].reshape(MLA_Q_LORA, MLA_HEADS, MLA_NOPE + MLA_ROPE)
            wq_n = wq[:, :, :MLA_NOPE].reshape(MLA_Q_LORA, MLA_HEADS * MLA_NOPE).astype(BF16)
            wq_p = jnp.pad(wq[:, :, MLA_NOPE:], ((0, 0), (0, 0), (0, LANE - MLA_ROPE)))
            wq_p = wq_p.reshape(MLA_Q_LORA, MLA_HEADS * LANE).astype(BF16)
            qn = norm_linear(cqkv, 0, MLA_Q_LORA, mla_q_norm_g[j], wq_n,
                             col_scale=jnp.full((MLA_HEADS * MLA_NOPE,), mla_scale, F32), tn=1024, **proj_rows)
            qp = norm_linear(cqkv, 0, MLA_Q_LORA, mla_q_norm_g[j], wq_p, rope=(96, 32),
                             tables=(c_t[1:2], s1_t[1:2], s2_t[1:2]), pos_tiles=pos_tiles, tn=512, **proj_rows)
            wkv = mla_w_kv_up[j].reshape(MLA_KV_LORA, MLA_HEADS, MLA_NOPE + MLA_V)
            w_kn = wkv[:, :, :MLA_NOPE].reshape(MLA_KV_LORA, -1).astype(BF16)
            w_v = wkv[:, :, MLA_NOPE:].reshape(MLA_KV_LORA, -1).astype(BF16)
            kn = norm_linear(cqkv, 1, MLA_KV_LORA, mla_kv_norm_g[j], w_kn, tn=1024, **proj_rows)
            vt = norm_linear(cqkv, 1, MLA_KV_LORA, mla_kv_norm_g[j], w_v, layout="heads_t", tn=1024, **proj_rows)
            ox = mla_attention(qn, qp, kn, kp, vt, batch=batch, seq=seq, ctx_len=ctx_len)
            if not last:
                oz = ctx_attention([(qn, lambda h: h), (qp, lambda h: h)],
                                   [(kn, lambda h: h), (kp, lambda h: 0)], (vt, lambda h: h),
                                   batch=batch, ctx_len=ctx_len, z0=z0, n_heads=MLA_HEADS, v_transposed=True)
            w_o = mla_w_o[j]
        else:
            c_t, s_t = swa_rope_tables(seq, qk_scale, tp)
            n_q_blocks = SWA_HEADS * HEAD_DIM // 512
            qkv = norm_linear(stream, 0, d, norm_g[li, 1], swa_w_qkv[j].astype(BF16), rope=(64,),
                              tables=(c_t, s_t), set_thresholds=(n_q_blocks, n_q_blocks + 1),
                              pos_tiles=pos_tiles, tn=512, **proj)
            ox = swa_attention(qkv, swa_sink[j], batch=batch, seq=seq, ctx_len=ctx_len)
            if not last:
                g = SWA_HEADS // SWA_KV_HEADS
                oz = ctx_attention([(qkv, lambda h: h)], [(qkv, lambda h: SWA_HEADS + h // g)],
                                   (qkv, lambda h: SWA_HEADS + SWA_KV_HEADS + h // g),
                                   batch=batch, ctx_len=ctx_len, z0=z0, n_heads=SWA_HEADS,
                                   sink=swa_sink[j])
            w_o = swa_w_o[j]

        n_rows = nx if last else nx + nz
        stream = linear_residual(ox, w_o.astype(BF16), stream, mods, k_gate=5, n_tiles=n_rows // to_,
                                 a_tail=None if last else oz, tiles_per_group=seq // to_, n_groups=n_groups)
        stream = ffn_half_step_f32_weights(stream, norm_g[li, 2], mods, ffn_w_in, ffn_w_out, (li, 1), k0=6,
                                           total_tiles=n_rows // tf_,
                                           final_gain=final_norm_g if last else None, **ffn_rows)

    return stream.reshape(batch, seq, d)
```

```python
import functools
import math

import jax
import jax.numpy as jnp
from jax import lax
from jax.experimental import pallas as pl
from jax.experimental.pallas import tpu as pltpu

DEPTH = 4
GRID_W = 64
N_MIXERS = 3
N_MOD = 9
RMS_EPS = 1e-6
ROPE_BASE = 10000.0
NEG_INF = -1e30
LOG2E = math.log2(math.e)

NA_HEADS = 16
NA_KH = 8
NA_KW = 16
HEAD_DIM = 128

MLA_HEADS = 16
MLA_Q_LORA = 512
MLA_KV_LORA = 512
MLA_NOPE = 128
MLA_ROPE = 64
MLA_V = 128

SWA_HEADS = 16
SWA_KV_HEADS = 4
SWA_WINDOW = 128

LANE = 128
FFN_ROW_TILE = 1024
FFN_VMEM_LIMIT = 62 * 1024 * 1024
OUT_ROW_TILE = 512
PROJ_ROW_TILE = 1024
FFN_CHUNK = 512
FFN_CAST_CHUNK = 256
FFN_NORM_SLABS = 4
ROW_CHUNK = 128
VMEM_LIMIT = 56 * 1024 * 1024

F32 = jnp.float32
BF16 = jnp.bfloat16


def _cparams(*sem):
    return pltpu.CompilerParams(dimension_semantics=sem, vmem_limit_bytes=VMEM_LIMIT)


def _dot(a, b):
    return jnp.dot(a, b, preferred_element_type=F32)


def _dot_t(a, b):
    return lax.dot_general(a, b, (((1,), (1,)), ((), ())), preferred_element_type=F32)


def _sigmoid(x):
    return 1.0 / (1.0 + jnp.exp(-x))


def _rmsnorm_rows(x, g):
    ms = jnp.mean(x * x, axis=-1, keepdims=True)
    return x * lax.rsqrt(ms + RMS_EPS) * g


def _modulated_norm(read_x, g_ref, mod_ref, k_shift, k_scale):
    x = read_x()
    r = lax.rsqrt(jnp.mean(x * x, axis=-1, keepdims=True) + RMS_EPS)
    gs = g_ref[...] * (1.0 + mod_ref[0, k_scale:k_scale + 1, :])
    return (read_x() * r) * gs + mod_ref[0, k_shift:k_shift + 1, :]


def _group_fn(tiles_per_group, n_groups):
    return lambda i: jnp.minimum(i // tiles_per_group, n_groups - 1)


def _two_source_specs(block, split_tiles):
    return [pl.BlockSpec(block, lambda i, j: (jnp.minimum(i, split_tiles - 1), 0)),
            pl.BlockSpec(block, lambda i, j: (jnp.maximum(i - split_tiles, 0), 0))]


def _read_rows(ref, tail_ref, split_tiles, rows=slice(None), tile=None):
    if tail_ref is None:
        return ref[rows, :]
    tile = pl.program_id(0) if tile is None else tile
    return jnp.where(tile < split_tiles, ref[rows, :], tail_ref[rows, :])


def _for_row_chunks(n_rows, body):
    def step(c, carry):
        body(pl.ds(pl.multiple_of(c * ROW_CHUNK, ROW_CHUNK), ROW_CHUNK))
        return carry

    lax.fori_loop(0, n_rows // ROW_CHUNK, step, 0)


def _mod_kernel(c_ref, w_ref, b_ref, o_ref):
    c = c_ref[...]
    a = (c * _sigmoid(c)).astype(BF16)
    o_ref[0] = _dot(a, w_ref[0].astype(BF16)) + b_ref[0]


def mod_params(cond, w_mod, b_mod):
    depth, d, n = w_mod.shape
    g = cond.shape[0]
    tn = math.gcd(n, 1024)
    return pl.pallas_call(
        _mod_kernel,
        grid=(depth, n // tn),
        in_specs=[
            pl.BlockSpec((g, d), lambda l, j: (0, 0)),
            pl.BlockSpec((1, d, tn), lambda l, j: (l, 0, j)),
            pl.BlockSpec((1, 1, tn), lambda l, j: (l, 0, j)),
        ],
        out_specs=pl.BlockSpec((1, g, tn), lambda l, j: (l, 0, j)),
        out_shape=jax.ShapeDtypeStruct((depth, g, n), F32),
        compiler_params=_cparams("parallel", "parallel"),
        name="mod_params",
    )(cond, w_mod, b_mod.reshape(depth, 1, n))


def _ffn_kernel(*refs, k0, final_norm, split_tiles, tile_offset, cast_weights):
    it = iter(refs)
    x_ref = next(it)
    xt_ref = next(it) if split_tiles is not None else None
    g_ref, mod_ref, wg_ref, wu_ref, wo_ref = (next(it) for _ in range(5))
    gf_ref = next(it) if final_norm else None
    o_ref = next(it)
    w_bf_refs = [next(it) for _ in range(3)] if cast_weights else None
    h_ref = next(it)
    j = pl.program_id(1)
    tile = pl.program_id(0) + tile_offset

    def weights():
        wg, wu, wo = wg_ref[...], wu_ref[...], wo_ref[...]
        if cast_weights:
            wg, wu, wo = wg.astype(BF16), wu.astype(BF16), wo.astype(BF16)
            for ref, w in zip(w_bf_refs, (wg, wu, wo)):
                ref[...] = w
        return wg, wu, wo

    def swiglu_down(h, wg, wu, wo):
        gate = _dot(h, wg)
        up = _dot(h, wu)
        return _dot((gate * _sigmoid(gate) * up).astype(BF16), wo)

    @pl.when(j == 0)
    def _():
        wg, wu, wo = weights()
        n_rows = o_ref.shape[0]
        slab = n_rows // FFN_NORM_SLABS
        for q in range(FFN_NORM_SLABS):
            rows = slice(q * slab, (q + 1) * slab)
            read_x = lambda rows=rows: _read_rows(x_ref, xt_ref, split_tiles, rows, tile)
            h = _modulated_norm(read_x, g_ref, mod_ref, k0, k0 + 1).astype(BF16)
            h_ref[rows, :] = h
            o_ref[rows, :] = swiglu_down(h, wg, wu, wo)

    @pl.when(j > 0)
    def _():
        o_ref[...] += swiglu_down(h_ref[...], *weights())

    @pl.when(j == pl.num_programs(1) - 1)
    def _():
        def chunk(rows):
            x = _read_rows(x_ref, xt_ref, split_tiles, rows, tile)
            y = x + (0.5 * mod_ref[0, k0 + 2:k0 + 3, :]) * o_ref[rows, :]
            if final_norm:
                y = _rmsnorm_rows(y, gf_ref[...])
            o_ref[rows, :] = y

        _for_row_chunks(o_ref.shape[0], chunk)


def stacked_ffn_weights(w_in, w_out, lead, tf):
    d, f = w_in.shape[-2], w_out.shape[-2]
    nf = f // tf
    sq = (None,) * len(lead)
    return ((w_in, sq + (d, tf), lambda i, j: lead + (0, j)),
            (w_in, sq + (d, tf), lambda i, j: lead + (0, j + nf)),
            (w_out, sq + (tf, d), lambda i, j: lead + (j, 0)))


def plain_ffn_weights(w_gate, w_up, w_down, tf):
    d = w_gate.shape[0]
    return ((w_gate, (d, tf), lambda i, j: (0, j)),
            (w_up, (d, tf), lambda i, j: (0, j)),
            (w_down, (tf, d), lambda i, j: (j, 0)))


def ffn_half_step(x, gain, mods, weights, *, k0, n_tiles, total_tiles, tile_offset, tiles_per_group,
                  n_groups, tf, x_tail=None, final_gain=None, cast_weights=False, in_place=False,
                  tm=FFN_ROW_TILE):
    d = x.shape[1]
    f = weights[2][0].shape[-2]
    nf = f // tf
    grp = _group_fn(tiles_per_group, n_groups)
    split_tiles = None if x_tail is None else x.shape[0] // tm
    one_tile = n_tiles == 1
    if x_tail is None:
        in_specs = [pl.BlockSpec((tm, d), lambda i, j: (i + tile_offset, 0),
                                 **(dict(pipeline_mode=pl.Buffered(1)) if one_tile else {}))]
        args = [x]
    else:
        in_specs = [pl.BlockSpec((tm, d), lambda i, j: (jnp.minimum(i + tile_offset, split_tiles - 1), 0)),
                    pl.BlockSpec((tm, d), lambda i, j: (jnp.maximum(i + tile_offset - split_tiles, 0), 0))]
        args = [x, x_tail]
    in_specs += [pl.BlockSpec((1, d), lambda i, j: (0, 0)),
                 pl.BlockSpec((1, N_MOD, d), lambda i, j: (grp(i + tile_offset), 0, 0))]
    args += [gain.reshape(1, d), mods]
    for arr, block, index_map in weights:
        in_specs.append(pl.BlockSpec(block, index_map))
        args.append(arr)
    if final_gain is not None:
        in_specs.append(pl.BlockSpec((1, d), lambda i, j: (0, 0)))
        args.append(final_gain.reshape(1, d))
    if in_place:
        assert x_tail is None and x.shape[0] == total_tiles * tm
    out_specs = [pl.BlockSpec((tm, d), lambda i, j: (i + tile_offset, 0))]
    out_shape = [jax.ShapeDtypeStruct((total_tiles * tm, d), F32)]
    if cast_weights:
        out_specs += [pl.BlockSpec((d, tf), lambda i, j: (0, j)), pl.BlockSpec((d, tf), lambda i, j: (0, j)),
                      pl.BlockSpec((tf, d), lambda i, j: (j, 0))]
        out_shape += [jax.ShapeDtypeStruct((d, f), BF16), jax.ShapeDtypeStruct((d, f), BF16),
                      jax.ShapeDtypeStruct((f, d), BF16)]
    out = pl.pallas_call(
        functools.partial(_ffn_kernel, k0=k0, final_norm=final_gain is not None, split_tiles=split_tiles,
                          tile_offset=tile_offset, cast_weights=cast_weights),
        grid=(n_tiles, nf),
        in_specs=in_specs,
        out_specs=out_specs,
        out_shape=out_shape,
        input_output_aliases={0: 0} if in_place else {},
        scratch_shapes=[pltpu.VMEM((tm, d), BF16)],
        compiler_params=pltpu.CompilerParams(dimension_semantics=("parallel", "arbitrary"),
                                             vmem_limit_bytes=FFN_VMEM_LIMIT),
        name="ffn_half_step",
    )(*args)
    return out if cast_weights else out[0]


def ffn_half_step_f32_weights(x, gain, mods, w_in, w_out, lead, *, total_tiles, **kw):
    y, w_gate, w_up, w_down = ffn_half_step(
        x, gain, mods, stacked_ffn_weights(w_in, w_out, lead, FFN_CAST_CHUNK), n_tiles=1, total_tiles=total_tiles,
        tile_offset=0, tf=FFN_CAST_CHUNK, cast_weights=True, in_place=True, **kw)
    return ffn_half_step(
        y, gain, mods, plain_ffn_weights(w_gate, w_up, w_down, FFN_CHUNK), n_tiles=total_tiles - 1,
        total_tiles=total_tiles, tile_offset=1, tf=FFN_CHUNK, in_place=True, **kw)


def _norm_linear_kernel(*refs, mod_idx, rope, layout, n_chunks):
    it = iter(refs)
    x_ref, g_ref = next(it), next(it)
    mod_ref = next(it) if mod_idx is not None else None
    w_ref = next(it)
    if rope is None:
        cs_ref = next(it)
    else:
        tab_refs = [next(it) for _ in range(len(rope) + 1)]
    o_ref, h_ref = next(it), next(it)

    @pl.when(pl.program_id(1) == 0)
    def _():
        if mod_idx is None:
            y = _rmsnorm_rows(x_ref[...], g_ref[...])
        else:
            y = _modulated_norm(lambda: x_ref[...], g_ref, mod_ref, *mod_idx)
        h_ref[...] = y.astype(BF16)

    acc = _dot(h_ref[...], w_ref[...])
    if rope is None:
        acc = acc * cs_ref[...]
    for c in range(n_chunks):
        a = acc[:, c * LANE:(c + 1) * LANE]
        if rope is not None:
            out = a * tab_refs[0][0]
            for shift, tab in zip(rope, tab_refs[1:]):
                out = out + pltpu.roll(a, shift, 1) * tab[0]
            a = out
        if layout == "heads":
            o_ref[c] = a.astype(o_ref.dtype)
        elif layout == "heads_t":
            o_ref[c] = a.T.astype(o_ref.dtype)
        else:
            o_ref[:, c * LANE:(c + 1) * LANE] = a.astype(o_ref.dtype)


def norm_linear(x, x_col_block, k, gain, w, *, n_tiles, tiles_per_group, n_groups,
                mods=None, mod_idx=None, col_scale=None, rope=None, tables=None,
                set_thresholds=(), pos_tiles=None, layout="heads", out_dtype=BF16,
                tm=PROJ_ROW_TILE, tn=512):
    n = w.shape[1]
    nj = n // tn
    n_chunks = tn // LANE
    grp = _group_fn(tiles_per_group, n_groups)
    in_specs = [
        pl.BlockSpec((tm, k), lambda i, j: (i, x_col_block)),
        pl.BlockSpec((1, k), lambda i, j: (0, 0)),
    ]
    args = [x, gain.reshape(1, k)]
    if mod_idx is not None:
        in_specs.append(pl.BlockSpec((1, N_MOD, k), lambda i, j: (grp(i), 0, 0)))
        args.append(mods)
    in_specs.append(pl.BlockSpec((k, tn), lambda i, j: (0, j)))
    args.append(w)
    if rope is None:
        if col_scale is None:
            col_scale = jnp.ones((n,), F32)
        in_specs.append(pl.BlockSpec((1, tn), lambda i, j: (0, j)))
        args.append(col_scale.reshape(1, n))
    else:
        n_lat_tiles = tiles_per_group * (n_groups - 1)

        def tab_map(i, j):
            s = sum((j >= t).astype(jnp.int32) for t in set_thresholds) if set_thresholds else 0
            p = jnp.where(i < n_lat_tiles, i % pos_tiles, pos_tiles)
            return (s, p, 0)

        for t in tables:
            in_specs.append(pl.BlockSpec((1, tm, LANE), tab_map))
            args.append(t)
    rows = n_tiles * tm
    if layout == "heads":
        out_spec = pl.BlockSpec((n_chunks, tm, LANE), lambda i, j: (j, i, 0))
        out_shape = jax.ShapeDtypeStruct((n // LANE, rows, LANE), out_dtype)
    elif layout == "heads_t":
        out_spec = pl.BlockSpec((n_chunks, LANE, tm), lambda i, j: (j, 0, i))
        out_shape = jax.ShapeDtypeStruct((n // LANE, LANE, rows), out_dtype)
    else:
        out_spec = pl.BlockSpec((tm, tn), lambda i, j: (i, j))
        out_shape = jax.ShapeDtypeStruct((rows, n), out_dtype)
    return pl.pallas_call(
        functools.partial(_norm_linear_kernel, mod_idx=mod_idx, rope=rope,
                          layout=layout, n_chunks=n_chunks),
        grid=(n_tiles, nj),
        in_specs=in_specs,
        out_specs=out_spec,
        out_shape=out_shape,
        scratch_shapes=[pltpu.VMEM((tm, k), BF16)],
        compiler_params=_cparams("parallel", "arbitrary"),
        name="norm_linear",
    )(*args)


def _linear_residual_kernel(*refs, k_gate, split_tiles):
    it = iter(refs)
    a_ref = next(it)
    at_ref = next(it) if split_tiles is not None else None
    w_ref, x_ref, mod_ref, o_ref = next(it), next(it), next(it), next(it)
    acc = _dot(_read_rows(a_ref, at_ref, split_tiles), w_ref[...])
    o_ref[...] = x_ref[...] + mod_ref[0, k_gate:k_gate + 1, :] * acc


def linear_residual(a, w, x, mods, *, k_gate, n_tiles, tiles_per_group, n_groups, a_tail=None,
                    tm=OUT_ROW_TILE):
    kdim, n = w.shape
    grp = _group_fn(tiles_per_group, n_groups)
    split_tiles = None if a_tail is None else a.shape[0] // tm
    if a_tail is None:
        in_specs, args = [pl.BlockSpec((tm, kdim), lambda i, j: (i, 0))], [a]
    else:
        in_specs, args = _two_source_specs((tm, kdim), split_tiles), [a, a_tail]
    in_specs += [
        pl.BlockSpec((kdim, n), lambda i, j: (0, 0)),
        pl.BlockSpec((tm, n), lambda i, j: (i, 0)),
        pl.BlockSpec((1, N_MOD, n), lambda i, j: (grp(i), 0, 0)),
    ]
    args += [w, x, mods]
    return pl.pallas_call(
        functools.partial(_linear_residual_kernel, k_gate=k_gate, split_tiles=split_tiles),
        grid=(n_tiles, 1),
        in_specs=in_specs,
        out_specs=pl.BlockSpec((tm, n), lambda i, j: (i, 0)),
        out_shape=jax.ShapeDtypeStruct((n_tiles * tm, n), F32),
        compiler_params=_cparams("parallel", "arbitrary"),
        name="linear_residual",
    )(*args)


NA_QROWS = 4
NA_QTOK = NA_QROWS * GRID_W
NA_BAND_BLOCKS = 3
NA_PAIRS = 2 * NA_KH


def _na_bias_kernel(rpb_ref, o_ref):
    h = pl.program_id(0)
    cq = lax.broadcasted_iota(jnp.int32, (GRID_W, 2 * GRID_W), 0)
    lane = lax.broadcasted_iota(jnp.int32, (GRID_W, 2 * GRID_W), 1)
    second = lane >= GRID_W
    ck = jnp.where(second, lane - GRID_W, lane)
    n_dc = 2 * NA_KW - 1
    code = ck - cq + (NA_KW - 1) + jnp.where(second, n_dc, 0)
    start = jnp.clip(cq - NA_KW // 2, 0, GRID_W - NA_KW)
    in_win = (ck >= start) & (ck < start + NA_KW)
    for p in range(NA_PAIRS):
        tile = jnp.zeros((GRID_W, 2 * GRID_W), F32)
        for half in range(2):
            dr = p - NA_KH + half
            if not (-(NA_KH - 1) <= dr <= NA_KH - 1):
                continue
            for dc in range(n_dc):
                val = rpb_ref[h, dr + NA_KH - 1, dc]
                tile = jnp.where(code == half * n_dc + dc, val, tile)
        o_ref[0, p] = jnp.where(in_win, tile * LOG2E, NEG_INF)


def na_bias_table(rpb):
    heads = rpb.shape[0]
    return pl.pallas_call(
        _na_bias_kernel,
        grid=(heads,),
        in_specs=[pl.BlockSpec(memory_space=pltpu.SMEM)],
        out_specs=pl.BlockSpec((1, NA_PAIRS, GRID_W, 2 * GRID_W), lambda h: (h, 0, 0, 0)),
        out_shape=jax.ShapeDtypeStruct((heads, NA_PAIRS, GRID_W, 2 * GRID_W), F32),
        compiler_params=_cparams("parallel"),
        name="na_bias_table",
    )(rpb)


def _na_kernel(q_ref, k0_ref, k1_ref, k2_ref, v0_ref, v1_ref, v2_ref, kz_ref, vz_ref,
               t2_ref, o_ref, vm_ref, ob_ref, *, rows):
    qb = pl.program_id(1)
    r0 = NA_QROWS * qb
    u0 = jnp.clip(r0 - NA_KH // 2, 0, rows - NA_BAND_BLOCKS * NA_QROWS)
    k_refs = (k0_ref, k1_ref, k2_ref, kz_ref)
    v_refs = (v0_ref, v1_ref, v2_ref, vz_ref)

    q_row = r0 + (lax.broadcasted_iota(jnp.int32, (NA_QTOK, NA_QTOK), 0) >> 6)
    k_off = lax.broadcasted_iota(jnp.int32, (NA_QTOK, NA_QTOK), 1) >> 6
    rs = jnp.clip(q_row - NA_KH // 2, 0, rows - NA_KH)
    for i in range(NA_BAND_BLOCKS):
        rk = u0 + NA_QROWS * i + k_off
        vm_ref[i] = jnp.where((rk >= rs) & (rk < rs + NA_KH), 0.0, NEG_INF)

    def scores(h):
        return _dot_t(q_ref[h], jnp.concatenate([r[h] for r in k_refs], axis=0))

    def finish_head(h, s_all):
        s = []
        for i in range(NA_BAND_BLOCKS):
            bias_rows = []
            for qi in range(NA_QROWS):
                pair = []
                for c in range(2):
                    dr0 = u0 + NA_QROWS * i + 2 * c - (r0 + qi)
                    pair.append(t2_ref[h, jnp.clip(dr0 + NA_KH, 0, NA_PAIRS - 1)])
                bias_rows.append(jnp.concatenate(pair, axis=1))
            s.append(s_all[:, i * NA_QTOK:(i + 1) * NA_QTOK] + jnp.concatenate(bias_rows, axis=0) + vm_ref[i])
        s.append(s_all[:, NA_BAND_BLOCKS * NA_QTOK:])
        m = jnp.maximum(jnp.maximum(s[0], s[1]), jnp.maximum(s[2], s[3])).max(axis=-1, keepdims=True)
        p = jnp.concatenate([jnp.exp2(si - m) for si in s], axis=1).astype(BF16)
        v = jnp.concatenate([r[h] for r in v_refs], axis=0)
        o = _dot(p, jnp.concatenate([v, jnp.ones_like(v)], axis=1))
        ob_ref[h] = (o[:, :HEAD_DIM] * (1.0 / o[:, HEAD_DIM:])).astype(BF16)

    s_next = scores(0)
    for h in range(NA_HEADS):
        s_cur = s_next
        if h + 1 < NA_HEADS:
            s_next = scores(h + 1)
        finish_head(h, s_cur)
    for h in range(NA_HEADS):
        o_ref[:, h * HEAD_DIM:(h + 1) * HEAD_DIM] = ob_ref[h]


def na_attention(qkv, t2, *, batch, seq, ctx_len):
    rows = seq // GRID_W
    nqb = seq // NA_QTOK
    z0 = batch * seq // ctx_len
    assert ctx_len == NA_QTOK and rows >= NA_BAND_BLOCKS * NA_QROWS
    blk = (NA_HEADS, NA_QTOK, HEAD_DIM)

    def band(i):
        return lambda b, q: (0, b * nqb + jnp.clip(q - 1, 0, nqb - NA_BAND_BLOCKS) + i, 0)

    def shifted(fn, part):
        return lambda b, q: (part,) + fn(b, q)[1:]

    in_specs = [pl.BlockSpec(blk, lambda b, q: (0, b * nqb + q, 0))]
    in_specs += [pl.BlockSpec(blk, shifted(band(i), 1)) for i in range(NA_BAND_BLOCKS)]
    in_specs += [pl.BlockSpec(blk, shifted(band(i), 2)) for i in range(NA_BAND_BLOCKS)]
    in_specs += [pl.BlockSpec(blk, lambda b, q: (1, z0 + b, 0)),
                 pl.BlockSpec(blk, lambda b, q: (2, z0 + b, 0)),
                 pl.BlockSpec(t2.shape, lambda b, q: (0, 0, 0, 0))]
    return pl.pallas_call(
        functools.partial(_na_kernel, rows=rows),
        grid=(batch, nqb),
        in_specs=in_specs,
        out_specs=pl.BlockSpec((NA_QTOK, NA_HEADS * HEAD_DIM), lambda b, q: (b * nqb + q, 0)),
        out_shape=jax.ShapeDtypeStruct((batch * seq, NA_HEADS * HEAD_DIM), BF16),
        scratch_shapes=[pltpu.VMEM((NA_BAND_BLOCKS, NA_QTOK, NA_QTOK), F32),
                        pltpu.VMEM(blk, BF16)],
        compiler_params=_cparams("parallel", "arbitrary"),
        name="na_attention",
    )(*([qkv] * 9), t2)


MLA_TQ = 1024
MLA_TK = 1024


def _mla_kernel(qn_ref, qp_ref, kn_ref, kp_ref, vt_ref, knz_ref, kpz_ref, vtz_ref, o_ref, *, seq):
    q = jnp.concatenate([qn_ref[0], qp_ref[0]], axis=1)
    tq = q.shape[0]

    n_blocks = seq // MLA_TK

    def scores(kb):
        if kb == n_blocks:
            return _dot_t(jnp.concatenate([knz_ref[0], kpz_ref[0]], axis=1), q)
        rows = slice(kb * MLA_TK, (kb + 1) * MLA_TK)
        return _dot_t(jnp.concatenate([kn_ref[0, rows, :], kp_ref[0, rows, :]], axis=1), q)

    m = jnp.full((1, tq), NEG_INF, F32)
    l = jnp.zeros((1, tq), F32)
    acc = jnp.zeros((MLA_V, tq), F32)
    s_next = scores(0)
    for kb in range(n_blocks + 1):
        s = s_next
        if kb < n_blocks:
            s_next = scores(kb + 1)
            vt = vt_ref[0, :, kb * MLA_TK:(kb + 1) * MLA_TK]
        else:
            vt = vtz_ref[0]
        m_new = jnp.maximum(m, s.max(axis=0, keepdims=True))
        alpha = jnp.exp2(m - m_new)
        p = jnp.exp2(s - m_new)
        l = alpha * l + p.sum(axis=0, keepdims=True)
        acc = alpha * acc + _dot(vt, p.astype(BF16))
        m = m_new
    o_ref[...] = (acc * (1.0 / l)).T.astype(BF16)


def mla_attention(qn, qp, kn, kp, vt, *, batch, seq, ctx_len):
    h_ = MLA_HEADS
    nqb = seq // MLA_TQ
    z0 = batch * seq // ctx_len
    qblk = (1, MLA_TQ, LANE)
    kblk = (1, seq, LANE)
    zblk = (1, ctx_len, LANE)
    in_specs = [
        pl.BlockSpec(qblk, lambda b, h, q: (h, b * nqb + q, 0)),
        pl.BlockSpec(qblk, lambda b, h, q: (h, b * nqb + q, 0)),
        pl.BlockSpec(kblk, lambda b, h, q: (h, b, 0)),
        pl.BlockSpec(kblk, lambda b, h, q: (0, b, 0)),
        pl.BlockSpec((1, MLA_V, seq), lambda b, h, q: (h, 0, b)),
        pl.BlockSpec(zblk, lambda b, h, q: (h, z0 + b, 0)),
        pl.BlockSpec(zblk, lambda b, h, q: (0, z0 + b, 0)),
        pl.BlockSpec((1, MLA_V, ctx_len), lambda b, h, q: (h, 0, z0 + b)),
    ]
    return pl.pallas_call(
        functools.partial(_mla_kernel, seq=seq),
        grid=(batch, h_, nqb),
        in_specs=in_specs,
        out_specs=pl.BlockSpec((MLA_TQ, MLA_V), lambda b, h, q: (b * nqb + q, h)),
        out_shape=jax.ShapeDtypeStruct((batch * seq, h_ * MLA_V), BF16),
        compiler_params=_cparams("parallel", "parallel", "arbitrary"),
        name="mla_attention",
    )(qn, qp, kn, kp, vt, kn, kp, vt)


SWA_TQ = SWA_WINDOW


def _swa_kernel(sink_ref, q_ref, k0_ref, k1_ref, k2_ref, v0_ref, v1_ref, v2_ref,
                kz_ref, vz_ref, o_ref, *, n_blocks):
    qb = pl.program_id(1)
    g = SWA_HEADS // SWA_KV_HEADS
    m_rows = g * SWA_TQ
    row = lax.broadcasted_iota(jnp.int32, (m_rows, SWA_TQ), 0) & (SWA_TQ - 1)
    lane = lax.broadcasted_iota(jnp.int32, (m_rows, SWA_TQ), 1)
    mask_prev = (lane >= row) & (qb >= 1)
    mask_next = (lane <= row) & (qb <= n_blocks - 2)
    grp_id = lax.broadcasted_iota(jnp.int32, (m_rows, 1), 0) >> 7
    k_refs = (k0_ref, k1_ref, k2_ref, kz_ref)
    v_refs = (v0_ref, v1_ref, v2_ref, vz_ref)

    def scores(kvh):
        q4 = q_ref[kvh * g:(kvh + 1) * g].reshape(m_rows, HEAD_DIM)
        return _dot_t(q4, jnp.concatenate([r[kvh] for r in k_refs], axis=0))

    s_next = scores(0)
    for kvh in range(SWA_KV_HEADS):
        s_all = s_next
        if kvh + 1 < SWA_KV_HEADS:
            s_next = scores(kvh + 1)
        sink = jnp.zeros((m_rows, 1), F32)
        for gi in range(g):
            sink = jnp.where(grp_id == gi, sink_ref[kvh * g + gi] * LOG2E, sink)
        s = [jnp.where(mask_prev, s_all[:, :SWA_TQ], NEG_INF),
             s_all[:, SWA_TQ:2 * SWA_TQ],
             jnp.where(mask_next, s_all[:, 2 * SWA_TQ:3 * SWA_TQ], NEG_INF)]
        s += [s_all[:, c:c + SWA_TQ] for c in range(3 * SWA_TQ, s_all.shape[1], SWA_TQ)]
        mx = s[0]
        for si in s[1:]:
            mx = jnp.maximum(mx, si)
        m = jnp.maximum(mx.max(axis=-1, keepdims=True), sink)
        p = [jnp.exp2(si - m) for si in s]
        ps = p[0]
        for pi in p[1:]:
            ps = ps + pi
        l = ps.sum(axis=-1, keepdims=True) + jnp.exp2(sink - m)
        o = _dot(jnp.concatenate(p, axis=1).astype(BF16), jnp.concatenate([r[kvh] for r in v_refs], axis=0))
        o = (o * (1.0 / l)).astype(BF16)
        for gi in range(g):
            hq = kvh * g + gi
            o_ref[:, hq * HEAD_DIM:(hq + 1) * HEAD_DIM] = o[gi * SWA_TQ:(gi + 1) * SWA_TQ]


def swa_attention(qkv, sink, *, batch, seq, ctx_len):
    nb = seq // SWA_TQ
    kvh = SWA_KV_HEADS
    qpart = SWA_HEADS // kvh
    z0 = batch * seq // ctx_len
    kblk = (kvh, SWA_TQ, HEAD_DIM)
    zblk = (kvh, ctx_len, HEAD_DIM)
    assert ctx_len % SWA_TQ == 0

    def kmap(part, d):
        return lambda b, q: (part, b * nb + jnp.clip(q + d, 0, nb - 1), 0)

    in_specs = [pl.BlockSpec(memory_space=pltpu.SMEM),
                pl.BlockSpec((SWA_HEADS, SWA_TQ, HEAD_DIM), lambda b, q: (0, b * nb + q, 0))]
    in_specs += [pl.BlockSpec(kblk, kmap(qpart, d)) for d in (-1, 0, 1)]
    in_specs += [pl.BlockSpec(kblk, kmap(qpart + 1, d)) for d in (-1, 0, 1)]
    in_specs += [pl.BlockSpec(zblk, lambda b, q: (qpart, z0 + b, 0)),
                 pl.BlockSpec(zblk, lambda b, q: (qpart + 1, z0 + b, 0))]
    return pl.pallas_call(
        functools.partial(_swa_kernel, n_blocks=nb),
        grid=(batch, nb),
        in_specs=in_specs,
        out_specs=pl.BlockSpec((SWA_TQ, SWA_HEADS * HEAD_DIM), lambda b, q: (b * nb + q, 0)),
        out_shape=jax.ShapeDtypeStruct((batch * seq, SWA_HEADS * HEAD_DIM), BF16),
        compiler_params=_cparams("parallel", "arbitrary"),
        name="swa_attention",
    )(sink, *([qkv] * 9))


def _ctx_kernel(*refs, two_part, use_sink, v_transposed):
    it = iter(refs)
    sink_ref = next(it) if use_sink else None
    if two_part:
        q = jnp.concatenate([next(it)[0], next(it)[0]], axis=1)
        k = jnp.concatenate([next(it)[0], next(it)[0]], axis=1)
    else:
        q = next(it)[0]
        k = next(it)[0]
    v_ref, o_ref = next(it), next(it)
    s = _dot_t(q, k)
    m = s.max(axis=-1, keepdims=True)
    if use_sink:
        sink = sink_ref[pl.program_id(1)] * LOG2E
        m = jnp.maximum(m, sink)
    p = jnp.exp2(s - m)
    l = p.sum(axis=-1, keepdims=True)
    if use_sink:
        l = l + jnp.exp2(sink - m)
    pv = _dot_t(p.astype(BF16), v_ref[0]) if v_transposed else _dot(p.astype(BF16), v_ref[0])
    o_ref[...] = (pv * (1.0 / l)).astype(BF16)


def ctx_attention(q_parts, k_parts, v_part, *, batch, ctx_len, z0, n_heads, sink=None,
                  v_transposed=False):
    blk = (1, ctx_len, LANE)
    in_specs, args = [], []
    if sink is not None:
        in_specs.append(pl.BlockSpec(memory_space=pltpu.SMEM))
        args.append(sink)
    for arr, head_fn in (*q_parts, *k_parts):
        in_specs.append(pl.BlockSpec(blk, functools.partial(
            lambda b, h, head_fn: (head_fn(h), z0 + b, 0), head_fn=head_fn)))
        args.append(arr)
    v_arr, v_head = v_part
    if v_transposed:
        in_specs.append(pl.BlockSpec((1, LANE, ctx_len), lambda b, h: (v_head(h), 0, z0 + b)))
    else:
        in_specs.append(pl.BlockSpec(blk, lambda b, h: (v_head(h), z0 + b, 0)))
    args.append(v_arr)
    return pl.pallas_call(
        functools.partial(_ctx_kernel, two_part=len(q_parts) == 2, use_sink=sink is not None,
                          v_transposed=v_transposed),
        grid=(batch, n_heads),
        in_specs=in_specs,
        out_specs=pl.BlockSpec((ctx_len, LANE), lambda b, h: (b, h)),
        out_shape=jax.ShapeDtypeStruct((batch * ctx_len, n_heads * LANE), BF16),
        compiler_params=_cparams("parallel", "parallel"),
        name="ctx_attention",
    )(*args)


def _rope_angles(n_tokens, rot_dim):
    t = jnp.arange(n_tokens)
    row = (t // GRID_W).astype(F32)
    col = (t % GRID_W).astype(F32)
    n_freq = rot_dim // 4
    inv_freq = ROPE_BASE ** (-jnp.arange(n_freq, dtype=F32) / n_freq)
    ang = jnp.concatenate([row[:, None] * inv_freq, col[:, None] * inv_freq], axis=-1)
    return jnp.cos(ang), jnp.sin(ang)


def _with_identity_tile(tab, fill, tm):
    return jnp.concatenate([tab, jnp.broadcast_to(fill, (tm, LANE))], axis=0)


def swa_rope_tables(seq, scale, tm):
    cos, sin = _rope_angles(seq, HEAD_DIM)
    c = jnp.concatenate([cos, cos], axis=-1)
    s = jnp.concatenate([-sin, sin], axis=-1)
    one = jnp.ones((LANE,), F32)
    zero = jnp.zeros((LANE,), F32)
    ident_c = _with_identity_tile(jnp.ones_like(c), one, tm)
    ident_s = _with_identity_tile(jnp.zeros_like(s), zero, tm)
    c_sets = jnp.stack([_with_identity_tile(c * scale, one * scale, tm), _with_identity_tile(c, one, tm), ident_c])
    s_sets = jnp.stack([_with_identity_tile(s * scale, zero, tm), _with_identity_tile(s, zero, tm), ident_s])
    return c_sets, s_sets


def mla_rope_tables(seq, scale, tm):
    cos, sin = _rope_angles(seq, MLA_ROPE)
    half = MLA_ROPE // 2
    z = jnp.zeros((seq, half), F32)
    c = jnp.concatenate([cos, cos, z, z], axis=-1)
    s1 = jnp.concatenate([-sin, z, z, z], axis=-1)
    s2 = jnp.concatenate([z, sin, z, z], axis=-1)
    one = jnp.ones((LANE,), F32)
    zero = jnp.zeros((LANE,), F32)
    wt = _with_identity_tile
    c_sets = jnp.stack([wt(jnp.ones_like(c) * scale, one * scale, tm), wt(c * scale, one * scale, tm), wt(c, one, tm)])
    s1_sets = jnp.stack([wt(jnp.zeros_like(c), zero, tm), wt(s1 * scale, zero, tm), wt(s1, zero, tm)])
    s2_sets = jnp.stack([wt(jnp.zeros_like(c), zero, tm), wt(s2 * scale, zero, tm), wt(s2, zero, tm)])
    return c_sets, s1_sets, s2_sets


def kernel(x, c, ctx, c_ctx, w_mod, b_mod, norm_g, ffn_w_in, ffn_w_out, na_w_qkv, na_rpb, na_w_o,
           mla_w_down, mla_q_norm_g, mla_w_q_up, mla_kv_norm_g, mla_w_kv_up, mla_w_o,
           swa_w_qkv, swa_sink, swa_w_o, final_norm_g):
    batch, seq, d = x.shape
    ctx_len = ctx.shape[1]
    depth = w_mod.shape[0]
    tf_, tp, to_ = FFN_ROW_TILE, PROJ_ROW_TILE, OUT_ROW_TILE
    nx, nz = batch * seq, batch * ctx_len
    n_groups = batch + 1
    z0 = nx // ctx_len
    assert nx % ctx_len == 0 and all(seq % t == 0 and nz % t == 0 for t in (tf_, tp, to_))
    ffn_rows = dict(tiles_per_group=seq // tf_, n_groups=n_groups)
    proj_rows = dict(tiles_per_group=seq // tp, n_groups=n_groups, n_tiles=(nx + nz) // tp)
    pos_tiles = seq // tp

    n_cond = 8
    cond = jnp.concatenate([c, c_ctx[None, :], jnp.zeros((n_cond - batch - 1, d), F32)], axis=0)
    mods_all = mod_params(cond, w_mod, b_mod).reshape(depth, n_cond, N_MOD, d)

    qk_scale = HEAD_DIM ** -0.5 * LOG2E
    mla_scale = (MLA_NOPE + MLA_ROPE) ** -0.5 * LOG2E

    stream = None
    for li in range(depth):
        last = li == depth - 1
        j = li // N_MIXERS
        kind = li % N_MIXERS
        mods = mods_all[li]

        if li == 0:
            first_w = stacked_ffn_weights(ffn_w_in[0, 0].astype(BF16), ffn_w_out[0, 0].astype(BF16), (), FFN_CHUNK)
            all_tiles = (nx + nz) // to_
            stream = ffn_half_step(x.reshape(nx, d), norm_g[li, 0], mods, first_w, k0=0, n_tiles=all_tiles,
                                   total_tiles=all_tiles, tile_offset=0, tf=FFN_CHUNK, tm=to_,
                                   x_tail=ctx.reshape(nz, d), tiles_per_group=seq // to_, n_groups=n_groups)
        else:
            stream = ffn_half_step_f32_weights(stream, norm_g[li, 0], mods, ffn_w_in, ffn_w_out, (li, 0),
                                               k0=0, total_tiles=(nx + nz) // tf_, **ffn_rows)

        proj = dict(mods=mods, mod_idx=(3, 4), **proj_rows)
        if kind == 0:
            col_scale = jnp.concatenate([jnp.full((NA_HEADS * HEAD_DIM,), qk_scale, F32),
                                         jnp.ones((2 * NA_HEADS * HEAD_DIM,), F32)])
            qkv = norm_linear(stream, 0, d, norm_g[li, 1], na_w_qkv[j].astype(BF16),
                              col_scale=col_scale, tn=1024, **proj)
            t2 = na_bias_table(na_rpb[j])
            ox = na_attention(qkv, t2, batch=batch, seq=seq, ctx_len=ctx_len)
            if not last:
                hd = lambda part: (qkv, lambda h: part * NA_HEADS + h)
                oz = ctx_attention([hd(0)], [hd(1)], hd(2), batch=batch, ctx_len=ctx_len,
                                   z0=z0, n_heads=NA_HEADS)
            w_o = na_w_o[j]
        elif kind == 1:
            w_down = mla_w_down[j]
            lora = MLA_Q_LORA + MLA_KV_LORA
            cqkv = norm_linear(stream, 0, d, norm_g[li, 1], w_down[:, :lora].astype(BF16),
                               layout="rows", out_dtype=F32, tn=lora, **proj)
            c_t, s1_t, s2_t = mla_rope_tables(seq, mla_scale, tp)
            w_kpe = jnp.pad(w_down[:, lora:], ((0, 0), (0, LANE - MLA_ROPE))).astype(BF16)
            kp = norm_linear(stream, 0, d, norm_g[li, 1], w_kpe, rope=(96, 32),
                             tables=(c_t[2:], s1_t[2:], s2_t[2:]), pos_tiles=pos_tiles,
                             tn=LANE, **proj)
            wq = mla_w_q_up[j].reshape(MLA_Q_LORA, MLA_HEADS, MLA_NOPE + MLA_ROPE)
            wq_n = wq[:, :, :MLA_NOPE].reshape(MLA_Q_LORA, MLA_HEADS * MLA_NOPE).astype(BF16)
            wq_p = jnp.pad(wq[:, :, MLA_NOPE:], ((0, 0), (0, 0), (0, LANE - MLA_ROPE)))
            wq_p = wq_p.reshape(MLA_Q_LORA, MLA_HEADS * LANE).astype(BF16)
            qn = norm_linear(cqkv, 0, MLA_Q_LORA, mla_q_norm_g[j], wq_n,
                             col_scale=jnp.full((MLA_HEADS * MLA_NOPE,), mla_scale, F32), tn=1024, **proj_rows)
            qp = norm_linear(cqkv, 0, MLA_Q_LORA, mla_q_norm_g[j], wq_p, rope=(96, 32),
                             tables=(c_t[1:2], s1_t[1:2], s2_t[1:2]), pos_tiles=pos_tiles, tn=512, **proj_rows)
            wkv = mla_w_kv_up[j].reshape(MLA_KV_LORA, MLA_HEADS, MLA_NOPE + MLA_V)
            w_kn = wkv[:, :, :MLA_NOPE].reshape(MLA_KV_LORA, -1).astype(BF16)
            w_v = wkv[:, :, MLA_NOPE:].reshape(MLA_KV_LORA, -1).astype(BF16)
            kn = norm_linear(cqkv, 1, MLA_KV_LORA, mla_kv_norm_g[j], w_kn, tn=1024, **proj_rows)
            vt = norm_linear(cqkv, 1, MLA_KV_LORA, mla_kv_norm_g[j], w_v, layout="heads_t", tn=1024, **proj_rows)
            ox = mla_attention(qn, qp, kn, kp, vt, batch=batch, seq=seq, ctx_len=ctx_len)
            if not last:
                oz = ctx_attention([(qn, lambda h: h), (qp, lambda h: h)],
                                   [(kn, lambda h: h), (kp, lambda h: 0)], (vt, lambda h: h),
                                   batch=batch, ctx_len=ctx_len, z0=z0, n_heads=MLA_HEADS, v_transposed=True)
            w_o = mla_w_o[j]
        else:
            c_t, s_t = swa_rope_tables(seq, qk_scale, tp)
            n_q_blocks = SWA_HEADS * HEAD_DIM // 512
            qkv = norm_linear(stream, 0, d, norm_g[li, 1], swa_w_qkv[j].astype(BF16), rope=(64,),
                              tables=(c_t, s_t), set_thresholds=(n_q_blocks, n_q_blocks + 1),
                              pos_tiles=pos_tiles, tn=512, **proj)
            ox = swa_attention(qkv, swa_sink[j], batch=batch, seq=seq, ctx_len=ctx_len)
            if not last:
                g = SWA_HEADS // SWA_KV_HEADS
                oz = ctx_attention([(qkv, lambda h: h)], [(qkv, lambda h: SWA_HEADS + h // g)],
                                   (qkv, lambda h: SWA_HEADS + SWA_KV_HEADS + h // g),
                                   batch=batch, ctx_len=ctx_len, z0=z0, n_heads=SWA_HEADS,
                                   sink=swa_sink[j])
            w_o = swa_w_o[j]

        n_rows = nx if last else nx + nz
        stream = linear_residual(ox, w_o.astype(BF16), stream, mods, k_gate=5, n_tiles=n_rows // to_,
                                 a_tail=None if last else oz, tiles_per_group=seq // to_, n_groups=n_groups)
        stream = ffn_half_step_f32_weights(stream, norm_g[li, 2], mods, ffn_w_in, ffn_w_out, (li, 1), k0=6,
                                           total_tiles=n_rows // tf_,
                                           final_gain=final_norm_g if last else None, **ffn_rows)

    return stream.reshape(batch, seq, d)
```

```python
import functools
import math

import jax
import jax.numpy as jnp
from jax import lax
from jax.experimental import pallas as pl
from jax.experimental.pallas import tpu as pltpu

DEPTH = 4
GRID_W = 64
N_MIXERS = 3
N_MOD = 9
RMS_EPS = 1e-6
ROPE_BASE = 10000.0
NEG_INF = -1e30
LOG2E = math.log2(math.e)

NA_HEADS = 16
NA_KH = 8
NA_KW = 16
HEAD_DIM = 128

MLA_HEADS = 16
MLA_Q_LORA = 512
MLA_KV_LORA = 512
MLA_NOPE = 128
MLA_ROPE = 64
MLA_V = 128

SWA_HEADS = 16
SWA_KV_HEADS = 4
SWA_WINDOW = 128

LANE = 128
FFN_ROW_TILE = 1024
FFN_VMEM_LIMIT = 62 * 1024 * 1024
OUT_ROW_TILE = 512
PROJ_ROW_TILE = 1024
FFN_CHUNK = 512
FFN_CAST_CHUNK = 256
FFN_NORM_SLABS = 4
PROJ_NORM_SLABS = 4
VMEM_LIMIT = 56 * 1024 * 1024

F32 = jnp.float32
BF16 = jnp.bfloat16


def _cparams(*sem):
    return pltpu.CompilerParams(dimension_semantics=sem, vmem_limit_bytes=VMEM_LIMIT)


def _dot(a, b):
    return jnp.dot(a, b, preferred_element_type=F32)


def _dot_t(a, b):
    return lax.dot_general(a, b, (((1,), (1,)), ((), ())), preferred_element_type=F32)


def _sigmoid(x):
    return 1.0 / (1.0 + jnp.exp(-x))


def _rmsnorm_rows(x, g):
    ms = jnp.mean(x * x, axis=-1, keepdims=True)
    return x * lax.rsqrt(ms + RMS_EPS) * g


def _modulated_norm(read_x, g_ref, mod_ref, k_shift, k_scale):
    x = read_x()
    r = lax.rsqrt(jnp.mean(x * x, axis=-1, keepdims=True) + RMS_EPS)
    gs = g_ref[...] * (1.0 + mod_ref[0, k_scale:k_scale + 1, :])
    return (read_x() * r) * gs + mod_ref[0, k_shift:k_shift + 1, :]


def _group_fn(tiles_per_group, n_groups):
    return lambda i: jnp.minimum(i // tiles_per_group, n_groups - 1)


def _two_source_specs(block, split_tiles):
    return [pl.BlockSpec(block, lambda i, j: (jnp.minimum(i, split_tiles - 1), 0)),
            pl.BlockSpec(block, lambda i, j: (jnp.maximum(i - split_tiles, 0), 0))]


def _read_rows(ref, tail_ref, split_tiles, rows=slice(None), tile=None):
    if tail_ref is None:
        return ref[rows, :]
    tile = pl.program_id(0) if tile is None else tile
    return jnp.where(tile < split_tiles, ref[rows, :], tail_ref[rows, :])


def _mod_kernel(c_ref, w_ref, b_ref, o_ref):
    c = c_ref[...]
    a = (c * _sigmoid(c)).astype(BF16)
    o_ref[0] = _dot(a, w_ref[0].astype(BF16)) + b_ref[0]


def mod_params(cond, w_mod, b_mod):
    depth, d, n = w_mod.shape
    g = cond.shape[0]
    tn = math.gcd(n, 1024)
    return pl.pallas_call(
        _mod_kernel,
        grid=(depth, n // tn),
        in_specs=[
            pl.BlockSpec((g, d), lambda l, j: (0, 0)),
            pl.BlockSpec((1, d, tn), lambda l, j: (l, 0, j)),
            pl.BlockSpec((1, 1, tn), lambda l, j: (l, 0, j)),
        ],
        out_specs=pl.BlockSpec((1, g, tn), lambda l, j: (l, 0, j)),
        out_shape=jax.ShapeDtypeStruct((depth, g, n), F32),
        compiler_params=_cparams("parallel", "parallel"),
        name="mod_params",
    )(cond, w_mod, b_mod.reshape(depth, 1, n))


def _ffn_kernel(*refs, k0, final_norm, split_tiles, tile_offset, cast_weights, n_steps):
    it = iter(refs)
    x_ref = next(it)
    xt_ref = next(it) if split_tiles is not None else None
    g_ref, mod_ref, wg_ref, wu_ref, wo_ref = (next(it) for _ in range(5))
    gf_ref = next(it) if final_norm else None
    o_ref = next(it)
    w_bf_refs = [next(it) for _ in range(3)] if cast_weights else None
    h_ref = next(it)
    j = pl.program_id(1)
    tile = pl.program_id(0) + tile_offset

    def weights():
        wg, wu, wo = wg_ref[...], wu_ref[...], wo_ref[...]
        if cast_weights:
            wg, wu, wo = wg.astype(BF16), wu.astype(BF16), wo.astype(BF16)
            for ref, w in zip(w_bf_refs, (wg, wu, wo)):
                ref[...] = w
        return wg, wu, wo

    def swiglu_down(h, wg, wu, wo):
        gate = _dot(h, wg)
        up = _dot(h, wu)
        return _dot((gate * _sigmoid(gate) * up).astype(BF16), wo)

    slab = o_ref.shape[0] // FFN_NORM_SLABS
    slabs = [slice(q * slab, (q + 1) * slab) for q in range(FFN_NORM_SLABS)]

    @pl.when(j == 0)
    def _():
        wg, wu, wo = weights()
        for rows in slabs:
            read_x = lambda rows=rows: _read_rows(x_ref, xt_ref, split_tiles, rows, tile)
            h = _modulated_norm(read_x, g_ref, mod_ref, k0, k0 + 1).astype(BF16)
            h_ref[rows, :] = h
            o_ref[rows, :] = swiglu_down(h, wg, wu, wo)

    @pl.when((j > 0) & (j < n_steps - 1))
    def _():
        o_ref[...] += swiglu_down(h_ref[...], *weights())

    @pl.when(j == n_steps - 1)
    def _():
        wg, wu, wo = weights()
        for rows in slabs:
            acc = o_ref[rows, :] + swiglu_down(h_ref[rows, :], wg, wu, wo)
            y = _read_rows(x_ref, xt_ref, split_tiles, rows, tile) + (0.5 * mod_ref[0, k0 + 2:k0 + 3, :]) * acc
            if final_norm:
                y = _rmsnorm_rows(y, gf_ref[...])
            o_ref[rows, :] = y


def stacked_ffn_weights(w_in, w_out, lead, tf):
    d, f = w_in.shape[-2], w_out.shape[-2]
    nf = f // tf
    sq = (None,) * len(lead)
    return ((w_in, sq + (d, tf), lambda i, j: lead + (0, j)),
            (w_in, sq + (d, tf), lambda i, j: lead + (0, j + nf)),
            (w_out, sq + (tf, d), lambda i, j: lead + (j, 0)))


def plain_ffn_weights(w_gate, w_up, w_down, tf):
    d = w_gate.shape[0]
    return ((w_gate, (d, tf), lambda i, j: (0, j)),
            (w_up, (d, tf), lambda i, j: (0, j)),
            (w_down, (tf, d), lambda i, j: (j, 0)))


def ffn_half_step(x, gain, mods, weights, *, k0, n_tiles, total_tiles, tile_offset, tiles_per_group,
                  n_groups, tf, x_tail=None, final_gain=None, cast_weights=False, in_place=False,
                  tm=FFN_ROW_TILE):
    d = x.shape[1]
    f = weights[2][0].shape[-2]
    nf = f // tf
    assert nf >= 2
    grp = _group_fn(tiles_per_group, n_groups)
    split_tiles = None if x_tail is None else x.shape[0] // tm
    one_tile = n_tiles == 1
    if x_tail is None:
        in_specs = [pl.BlockSpec((tm, d), lambda i, j: (i + tile_offset, 0),
                                 **(dict(pipeline_mode=pl.Buffered(1)) if one_tile else {}))]
        args = [x]
    else:
        in_specs = [pl.BlockSpec((tm, d), lambda i, j: (jnp.minimum(i + tile_offset, split_tiles - 1), 0)),
                    pl.BlockSpec((tm, d), lambda i, j: (jnp.maximum(i + tile_offset - split_tiles, 0), 0))]
        args = [x, x_tail]
    in_specs += [pl.BlockSpec((1, d), lambda i, j: (0, 0)),
                 pl.BlockSpec((1, N_MOD, d), lambda i, j: (grp(i + tile_offset), 0, 0))]
    args += [gain.reshape(1, d), mods]
    for arr, block, index_map in weights:
        in_specs.append(pl.BlockSpec(block, index_map))
        args.append(arr)
    if final_gain is not None:
        in_specs.append(pl.BlockSpec((1, d), lambda i, j: (0, 0)))
        args.append(final_gain.reshape(1, d))
    if in_place:
        assert x_tail is None and x.shape[0] == total_tiles * tm
    out_specs = [pl.BlockSpec((tm, d), lambda i, j: (i + tile_offset, 0))]
    out_shape = [jax.ShapeDtypeStruct((total_tiles * tm, d), F32)]
    if cast_weights:
        out_specs += [pl.BlockSpec((d, tf), lambda i, j: (0, j)), pl.BlockSpec((d, tf), lambda i, j: (0, j)),
                      pl.BlockSpec((tf, d), lambda i, j: (j, 0))]
        out_shape += [jax.ShapeDtypeStruct((d, f), BF16), jax.ShapeDtypeStruct((d, f), BF16),
                      jax.ShapeDtypeStruct((f, d), BF16)]
    out = pl.pallas_call(
        functools.partial(_ffn_kernel, k0=k0, final_norm=final_gain is not None, split_tiles=split_tiles,
                          tile_offset=tile_offset, cast_weights=cast_weights, n_steps=nf),
        grid=(n_tiles, nf),
        in_specs=in_specs,
        out_specs=out_specs,
        out_shape=out_shape,
        input_output_aliases={0: 0} if in_place else {},
        scratch_shapes=[pltpu.VMEM((tm, d), BF16)],
        compiler_params=pltpu.CompilerParams(dimension_semantics=("parallel", "arbitrary"),
                                             vmem_limit_bytes=FFN_VMEM_LIMIT),
        name="ffn_half_step",
    )(*args)
    return out if cast_weights else out[0]


def ffn_half_step_f32_weights(x, gain, mods, w_in, w_out, lead, *, total_tiles, **kw):
    y, w_gate, w_up, w_down = ffn_half_step(
        x, gain, mods, stacked_ffn_weights(w_in, w_out, lead, FFN_CAST_CHUNK), n_tiles=1, total_tiles=total_tiles,
        tile_offset=0, tf=FFN_CAST_CHUNK, cast_weights=True, in_place=True, **kw)
    return ffn_half_step(
        y, gain, mods, plain_ffn_weights(w_gate, w_up, w_down, FFN_CHUNK), n_tiles=total_tiles - 1,
        total_tiles=total_tiles, tile_offset=1, tf=FFN_CHUNK, in_place=True, **kw)


def _norm_linear_kernel(*refs, mod_idx, rope, layout, n_chunks):
    it = iter(refs)
    x_ref, g_ref = next(it), next(it)
    mod_ref = next(it) if mod_idx is not None else None
    w_ref = next(it)
    if rope is None:
        cs_ref = next(it)
    else:
        tab_refs = [next(it) for _ in range(len(rope) + 1)]
    o_ref, h_ref = next(it), next(it)

    def project(h, rows):
        acc = _dot(h, w_ref[...])
        if rope is None:
            acc = acc * cs_ref[...]
        for c in range(n_chunks):
            a = acc[:, c * LANE:(c + 1) * LANE]
            if rope is not None:
                out = a * tab_refs[0][0, rows, :]
                for shift, tab in zip(rope, tab_refs[1:]):
                    out = out + pltpu.roll(a, shift, 1) * tab[0, rows, :]
                a = out
            if layout == "heads":
                o_ref[c, rows, :] = a.astype(o_ref.dtype)
            elif layout == "heads_t":
                o_ref[c, :, rows] = a.T.astype(o_ref.dtype)
            else:
                o_ref[rows, c * LANE:(c + 1) * LANE] = a.astype(o_ref.dtype)

    @pl.when(pl.program_id(1) == 0)
    def _():
        slab = x_ref.shape[0] // PROJ_NORM_SLABS
        for q in range(PROJ_NORM_SLABS):
            rows = slice(q * slab, (q + 1) * slab)
            if mod_idx is None:
                y = _rmsnorm_rows(x_ref[rows, :], g_ref[...])
            else:
                y = _modulated_norm(lambda rows=rows: x_ref[rows, :], g_ref, mod_ref, *mod_idx)
            h = y.astype(BF16)
            h_ref[rows, :] = h
            project(h, rows)

    @pl.when(pl.program_id(1) > 0)
    def _():
        project(h_ref[...], slice(None))


def norm_linear(x, x_col_block, k, gain, w, *, n_tiles, tiles_per_group, n_groups,
                mods=None, mod_idx=None, col_scale=None, rope=None, tables=None,
                set_thresholds=(), pos_tiles=None, layout="heads", out_dtype=BF16,
                tm=PROJ_ROW_TILE, tn=512):
    n = w.shape[1]
    nj = n // tn
    n_chunks = tn // LANE
    grp = _group_fn(tiles_per_group, n_groups)
    in_specs = [
        pl.BlockSpec((tm, k), lambda i, j: (i, x_col_block)),
        pl.BlockSpec((1, k), lambda i, j: (0, 0)),
    ]
    args = [x, gain.reshape(1, k)]
    if mod_idx is not None:
        in_specs.append(pl.BlockSpec((1, N_MOD, k), lambda i, j: (grp(i), 0, 0)))
        args.append(mods)
    in_specs.append(pl.BlockSpec((k, tn), lambda i, j: (0, j)))
    args.append(w)
    if rope is None:
        if col_scale is None:
            col_scale = jnp.ones((n,), F32)
        in_specs.append(pl.BlockSpec((1, tn), lambda i, j: (0, j)))
        args.append(col_scale.reshape(1, n))
    else:
        n_lat_tiles = tiles_per_group * (n_groups - 1)

        def tab_map(i, j):
            s = sum((j >= t).astype(jnp.int32) for t in set_thresholds) if set_thresholds else 0
            p = jnp.where(i < n_lat_tiles, i % pos_tiles, pos_tiles)
            return (s, p, 0)

        for t in tables:
            in_specs.append(pl.BlockSpec((1, tm, LANE), tab_map))
            args.append(t)
    rows = n_tiles * tm
    if layout == "heads":
        out_spec = pl.BlockSpec((n_chunks, tm, LANE), lambda i, j: (j, i, 0))
        out_shape = jax.ShapeDtypeStruct((n // LANE, rows, LANE), out_dtype)
    elif layout == "heads_t":
        out_spec = pl.BlockSpec((n_chunks, LANE, tm), lambda i, j: (j, 0, i))
        out_shape = jax.ShapeDtypeStruct((n // LANE, LANE, rows), out_dtype)
    else:
        out_spec = pl.BlockSpec((tm, tn), lambda i, j: (i, j))
        out_shape = jax.ShapeDtypeStruct((rows, n), out_dtype)
    return pl.pallas_call(
        functools.partial(_norm_linear_kernel, mod_idx=mod_idx, rope=rope,
                          layout=layout, n_chunks=n_chunks),
        grid=(n_tiles, nj),
        in_specs=in_specs,
        out_specs=out_spec,
        out_shape=out_shape,
        scratch_shapes=[pltpu.VMEM((tm, k), BF16)],
        compiler_params=_cparams("parallel", "arbitrary"),
        name="norm_linear",
    )(*args)


def _linear_residual_kernel(*refs, k_gate, split_tiles):
    it = iter(refs)
    a_ref = next(it)
    at_ref = next(it) if split_tiles is not None else None
    w_ref, x_ref, mod_ref, o_ref = next(it), next(it), next(it), next(it)
    acc = _dot(_read_rows(a_ref, at_ref, split_tiles), w_ref[...])
    o_ref[...] = x_ref[...] + mod_ref[0, k_gate:k_gate + 1, :] * acc


def linear_residual(a, w, x, mods, *, k_gate, n_tiles, tiles_per_group, n_groups, a_tail=None,
                    tm=OUT_ROW_TILE):
    kdim, n = w.shape
    grp = _group_fn(tiles_per_group, n_groups)
    split_tiles = None if a_tail is None else a.shape[0] // tm
    if a_tail is None:
        in_specs, args = [pl.BlockSpec((tm, kdim), lambda i, j: (i, 0))], [a]
    else:
        in_specs, args = _two_source_specs((tm, kdim), split_tiles), [a, a_tail]
    in_specs += [
        pl.BlockSpec((kdim, n), lambda i, j: (0, 0)),
        pl.BlockSpec((tm, n), lambda i, j: (i, 0)),
        pl.BlockSpec((1, N_MOD, n), lambda i, j: (grp(i), 0, 0)),
    ]
    args += [w, x, mods]
    return pl.pallas_call(
        functools.partial(_linear_residual_kernel, k_gate=k_gate, split_tiles=split_tiles),
        grid=(n_tiles, 1),
        in_specs=in_specs,
        out_specs=pl.BlockSpec((tm, n), lambda i, j: (i, 0)),
        out_shape=jax.ShapeDtypeStruct((n_tiles * tm, n), F32),
        compiler_params=_cparams("parallel", "arbitrary"),
        name="linear_residual",
    )(*args)


NA_QROWS = 4
NA_QTOK = NA_QROWS * GRID_W
NA_BAND_BLOCKS = 3
NA_PAIRS = 2 * NA_KH


def _na_bias_kernel(rpb_ref, o_ref):
    h = pl.program_id(0)
    cq = lax.broadcasted_iota(jnp.int32, (GRID_W, 2 * GRID_W), 0)
    lane = lax.broadcasted_iota(jnp.int32, (GRID_W, 2 * GRID_W), 1)
    second = lane >= GRID_W
    ck = jnp.where(second, lane - GRID_W, lane)
    n_dc = 2 * NA_KW - 1
    code = ck - cq + (NA_KW - 1) + jnp.where(second, n_dc, 0)
    start = jnp.clip(cq - NA_KW // 2, 0, GRID_W - NA_KW)
    in_win = (ck >= start) & (ck < start + NA_KW)
    for p in range(NA_PAIRS):
        tile = jnp.zeros((GRID_W, 2 * GRID_W), F32)
        for half in range(2):
            dr = p - NA_KH + half
            if not (-(NA_KH - 1) <= dr <= NA_KH - 1):
                continue
            for dc in range(n_dc):
                val = rpb_ref[h, dr + NA_KH - 1, dc]
                tile = jnp.where(code == half * n_dc + dc, val, tile)
        o_ref[0, p] = jnp.where(in_win, tile * LOG2E, NEG_INF)


def na_bias_table(rpb):
    heads = rpb.shape[0]
    return pl.pallas_call(
        _na_bias_kernel,
        grid=(heads,),
        in_specs=[pl.BlockSpec(memory_space=pltpu.SMEM)],
        out_specs=pl.BlockSpec((1, NA_PAIRS, GRID_W, 2 * GRID_W), lambda h: (h, 0, 0, 0)),
        out_shape=jax.ShapeDtypeStruct((heads, NA_PAIRS, GRID_W, 2 * GRID_W), F32),
        compiler_params=_cparams("parallel"),
        name="na_bias_table",
    )(rpb)


def _na_kernel(q_ref, k0_ref, k1_ref, k2_ref, v0_ref, v1_ref, v2_ref, kz_ref, vz_ref,
               t2_ref, o_ref, vm_ref, ob_ref, *, rows):
    qb = pl.program_id(1)
    r0 = NA_QROWS * qb
    u0 = jnp.clip(r0 - NA_KH // 2, 0, rows - NA_BAND_BLOCKS * NA_QROWS)
    k_refs = (k0_ref, k1_ref, k2_ref, kz_ref)
    v_refs = (v0_ref, v1_ref, v2_ref, vz_ref)

    q_row = r0 + (lax.broadcasted_iota(jnp.int32, (NA_QTOK, NA_QTOK), 0) >> 6)
    k_off = lax.broadcasted_iota(jnp.int32, (NA_QTOK, NA_QTOK), 1) >> 6
    rs = jnp.clip(q_row - NA_KH // 2, 0, rows - NA_KH)
    for i in range(NA_BAND_BLOCKS):
        rk = u0 + NA_QROWS * i + k_off
        vm_ref[i] = jnp.where((rk >= rs) & (rk < rs + NA_KH), 0.0, NEG_INF)

    def scores(h):
        return _dot_t(q_ref[h], jnp.concatenate([r[h] for r in k_refs], axis=0))

    def finish_head(h, s_all):
        s = []
        for i in range(NA_BAND_BLOCKS):
            bias_rows = []
            for qi in range(NA_QROWS):
                pair = []
                for c in range(2):
                    dr0 = u0 + NA_QROWS * i + 2 * c - (r0 + qi)
                    pair.append(t2_ref[h, jnp.clip(dr0 + NA_KH, 0, NA_PAIRS - 1)])
                bias_rows.append(jnp.concatenate(pair, axis=1))
            s.append(s_all[:, i * NA_QTOK:(i + 1) * NA_QTOK] + jnp.concatenate(bias_rows, axis=0) + vm_ref[i])
        s.append(s_all[:, NA_BAND_BLOCKS * NA_QTOK:])
        m = jnp.maximum(jnp.maximum(s[0], s[1]), jnp.maximum(s[2], s[3])).max(axis=-1, keepdims=True)
        p = jnp.concatenate([jnp.exp2(si - m) for si in s], axis=1).astype(BF16)
        v = jnp.concatenate([r[h] for r in v_refs], axis=0)
        o = _dot(p, jnp.concatenate([v, jnp.ones_like(v)], axis=1))
        ob_ref[h] = (o[:, :HEAD_DIM] * (1.0 / o[:, HEAD_DIM:])).astype(BF16)

    s_next = scores(0)
    for h in range(NA_HEADS):
        s_cur = s_next
        if h + 1 < NA_HEADS:
            s_next = scores(h + 1)
        finish_head(h, s_cur)
    for h in range(NA_HEADS):
        o_ref[:, h * HEAD_DIM:(h + 1) * HEAD_DIM] = ob_ref[h]


def na_attention(qkv, t2, *, batch, seq, ctx_len):
    rows = seq // GRID_W
    nqb = seq // NA_QTOK
    z0 = batch * seq // ctx_len
    assert ctx_len == NA_QTOK and rows >= NA_BAND_BLOCKS * NA_QROWS
    blk = (NA_HEADS, NA_QTOK, HEAD_DIM)

    def band(i):
        return lambda b, q: (0, b * nqb + jnp.clip(q - 1, 0, nqb - NA_BAND_BLOCKS) + i, 0)

    def shifted(fn, part):
        return lambda b, q: (part,) + fn(b, q)[1:]

    in_specs = [pl.BlockSpec(blk, lambda b, q: (0, b * nqb + q, 0))]
    in_specs += [pl.BlockSpec(blk, shifted(band(i), 1)) for i in range(NA_BAND_BLOCKS)]
    in_specs += [pl.BlockSpec(blk, shifted(band(i), 2)) for i in range(NA_BAND_BLOCKS)]
    in_specs += [pl.BlockSpec(blk, lambda b, q: (1, z0 + b, 0)),
                 pl.BlockSpec(blk, lambda b, q: (2, z0 + b, 0)),
                 pl.BlockSpec(t2.shape, lambda b, q: (0, 0, 0, 0))]
    return pl.pallas_call(
        functools.partial(_na_kernel, rows=rows),
        grid=(batch, nqb),
        in_specs=in_specs,
        out_specs=pl.BlockSpec((NA_QTOK, NA_HEADS * HEAD_DIM), lambda b, q: (b * nqb + q, 0)),
        out_shape=jax.ShapeDtypeStruct((batch * seq, NA_HEADS * HEAD_DIM), BF16),
        scratch_shapes=[pltpu.VMEM((NA_BAND_BLOCKS, NA_QTOK, NA_QTOK), F32),
                        pltpu.VMEM(blk, BF16)],
        compiler_params=_cparams("parallel", "arbitrary"),
        name="na_attention",
    )(*([qkv] * 9), t2)


MLA_TQ = 1024
MLA_TK = 1024


def _mla_kernel(qn_ref, qp_ref, kn_ref, kp_ref, vt_ref, knz_ref, kpz_ref, vtz_ref, o_ref, *, seq):
    q = jnp.concatenate([qn_ref[0], qp_ref[0]], axis=1)
    tq = q.shape[0]

    n_blocks = seq // MLA_TK

    def scores(kb):
        if kb == n_blocks:
            return _dot_t(jnp.concatenate([knz_ref[0], kpz_ref[0]], axis=1), q)
        rows = slice(kb * MLA_TK, (kb + 1) * MLA_TK)
        return _dot_t(jnp.concatenate([kn_ref[0, rows, :], kp_ref[0, rows, :]], axis=1), q)

    m = jnp.full((1, tq), NEG_INF, F32)
    l = jnp.zeros((1, tq), F32)
    acc = jnp.zeros((MLA_V, tq), F32)
    s_next = scores(0)
    for kb in range(n_blocks + 1):
        s = s_next
        if kb < n_blocks:
            s_next = scores(kb + 1)
            vt = vt_ref[0, :, kb * MLA_TK:(kb + 1) * MLA_TK]
        else:
            vt = vtz_ref[0]
        m_new = jnp.maximum(m, s.max(axis=0, keepdims=True))
        alpha = jnp.exp2(m - m_new)
        p = jnp.exp2(s - m_new)
        l = alpha * l + p.sum(axis=0, keepdims=True)
        acc = alpha * acc + _dot(vt, p.astype(BF16))
        m = m_new
    o_ref[...] = (acc * (1.0 / l)).T.astype(BF16)


def mla_attention(qn, qp, kn, kp, vt, *, batch, seq, ctx_len):
    h_ = MLA_HEADS
    nqb = seq // MLA_TQ
    z0 = batch * seq // ctx_len
    qblk = (1, MLA_TQ, LANE)
    kblk = (1, seq, LANE)
    zblk = (1, ctx_len, LANE)
    in_specs = [
        pl.BlockSpec(qblk, lambda b, h, q: (h, b * nqb + q, 0)),
        pl.BlockSpec(qblk, lambda b, h, q: (h, b * nqb + q, 0)),
        pl.BlockSpec(kblk, lambda b, h, q: (h, b, 0)),
        pl.BlockSpec(kblk, lambda b, h, q: (0, b, 0)),
        pl.BlockSpec((1, MLA_V, seq), lambda b, h, q: (h, 0, b)),
        pl.BlockSpec(zblk, lambda b, h, q: (h, z0 + b, 0)),
        pl.BlockSpec(zblk, lambda b, h, q: (0, z0 + b, 0)),
        pl.BlockSpec((1, MLA_V, ctx_len), lambda b, h, q: (h, 0, z0 + b)),
    ]
    return pl.pallas_call(
        functools.partial(_mla_kernel, seq=seq),
        grid=(batch, h_, nqb),
        in_specs=in_specs,
        out_specs=pl.BlockSpec((MLA_TQ, MLA_V), lambda b, h, q: (b * nqb + q, h)),
        out_shape=jax.ShapeDtypeStruct((batch * seq, h_ * MLA_V), BF16),
        compiler_params=_cparams("parallel", "parallel", "arbitrary"),
        name="mla_attention",
    )(qn, qp, kn, kp, vt, kn, kp, vt)


SWA_TQ = SWA_WINDOW


def _swa_kernel(sink_ref, q_ref, k0_ref, k1_ref, k2_ref, v0_ref, v1_ref, v2_ref,
                kz_ref, vz_ref, o_ref, *, n_blocks):
    qb = pl.program_id(1)
    g = SWA_HEADS // SWA_KV_HEADS
    m_rows = g * SWA_TQ
    row = lax.broadcasted_iota(jnp.int32, (m_rows, SWA_TQ), 0) & (SWA_TQ - 1)
    lane = lax.broadcasted_iota(jnp.int32, (m_rows, SWA_TQ), 1)
    mask_prev = (lane >= row) & (qb >= 1)
    mask_next = (lane <= row) & (qb <= n_blocks - 2)
    grp_id = lax.broadcasted_iota(jnp.int32, (m_rows, 1), 0) >> 7
    k_refs = (k0_ref, k1_ref, k2_ref, kz_ref)
    v_refs = (v0_ref, v1_ref, v2_ref, vz_ref)

    def scores(kvh):
        q4 = q_ref[kvh * g:(kvh + 1) * g].reshape(m_rows, HEAD_DIM)
        return _dot_t(q4, jnp.concatenate([r[kvh] for r in k_refs], axis=0))

    s_next = scores(0)
    for kvh in range(SWA_KV_HEADS):
        s_all = s_next
        if kvh + 1 < SWA_KV_HEADS:
            s_next = scores(kvh + 1)
        sink = jnp.zeros((m_rows, 1), F32)
        for gi in range(g):
            sink = jnp.where(grp_id == gi, sink_ref[kvh * g + gi] * LOG2E, sink)
        s = [jnp.where(mask_prev, s_all[:, :SWA_TQ], NEG_INF),
             s_all[:, SWA_TQ:2 * SWA_TQ],
             jnp.where(mask_next, s_all[:, 2 * SWA_TQ:3 * SWA_TQ], NEG_INF)]
        s += [s_all[:, c:c + SWA_TQ] for c in range(3 * SWA_TQ, s_all.shape[1], SWA_TQ)]
        mx = s[0]
        for si in s[1:]:
            mx = jnp.maximum(mx, si)
        m = jnp.maximum(mx.max(axis=-1, keepdims=True), sink)
        p = [jnp.exp2(si - m) for si in s]
        ps = p[0]
        for pi in p[1:]:
            ps = ps + pi
        l = ps.sum(axis=-1, keepdims=True) + jnp.exp2(sink - m)
        o = _dot(jnp.concatenate(p, axis=1).astype(BF16), jnp.concatenate([r[kvh] for r in v_refs], axis=0))
        o = (o * (1.0 / l)).astype(BF16)
        for gi in range(g):
            hq = kvh * g + gi
            o_ref[:, hq * HEAD_DIM:(hq + 1) * HEAD_DIM] = o[gi * SWA_TQ:(gi + 1) * SWA_TQ]


def swa_attention(qkv, sink, *, batch, seq, ctx_len):
    nb = seq // SWA_TQ
    kvh = SWA_KV_HEADS
    qpart = SWA_HEADS // kvh
    z0 = batch * seq // ctx_len
    kblk = (kvh, SWA_TQ, HEAD_DIM)
    zblk = (kvh, ctx_len, HEAD_DIM)
    assert ctx_len % SWA_TQ == 0

    def kmap(part, d):
        return lambda b, q: (part, b * nb + jnp.clip(q + d, 0, nb - 1), 0)

    in_specs = [pl.BlockSpec(memory_space=pltpu.SMEM),
                pl.BlockSpec((SWA_HEADS, SWA_TQ, HEAD_DIM), lambda b, q: (0, b * nb + q, 0))]
    in_specs += [pl.BlockSpec(kblk, kmap(qpart, d)) for d in (-1, 0, 1)]
    in_specs += [pl.BlockSpec(kblk, kmap(qpart + 1, d)) for d in (-1, 0, 1)]
    in_specs += [pl.BlockSpec(zblk, lambda b, q: (qpart, z0 + b, 0)),
                 pl.BlockSpec(zblk, lambda b, q: (qpart + 1, z0 + b, 0))]
    return pl.pallas_call(
        functools.partial(_swa_kernel, n_blocks=nb),
        grid=(batch, nb),
        in_specs=in_specs,
        out_specs=pl.BlockSpec((SWA_TQ, SWA_HEADS * HEAD_DIM), lambda b, q: (b * nb + q, 0)),
        out_shape=jax.ShapeDtypeStruct((batch * seq, SWA_HEADS * HEAD_DIM), BF16),
        compiler_params=_cparams("parallel", "arbitrary"),
        name="swa_attention",
    )(sink, *([qkv] * 9))


def _ctx_kernel(*refs, two_part, use_sink, v_transposed):
    it = iter(refs)
    sink_ref = next(it) if use_sink else None
    if two_part:
        q = jnp.concatenate([next(it)[0], next(it)[0]], axis=1)
        k = jnp.concatenate([next(it)[0], next(it)[0]], axis=1)
    else:
        q = next(it)[0]
        k = next(it)[0]
    v_ref, o_ref = next(it), next(it)
    s = _dot_t(q, k)
    m = s.max(axis=-1, keepdims=True)
    if use_sink:
        sink = sink_ref[pl.program_id(1)] * LOG2E
        m = jnp.maximum(m, sink)
    p = jnp.exp2(s - m)
    l = p.sum(axis=-1, keepdims=True)
    if use_sink:
        l = l + jnp.exp2(sink - m)
    pv = _dot_t(p.astype(BF16), v_ref[0]) if v_transposed else _dot(p.astype(BF16), v_ref[0])
    o_ref[...] = (pv * (1.0 / l)).astype(BF16)


def ctx_attention(q_parts, k_parts, v_part, *, batch, ctx_len, z0, n_heads, sink=None,
                  v_transposed=False):
    blk = (1, ctx_len, LANE)
    in_specs, args = [], []
    if sink is not None:
        in_specs.append(pl.BlockSpec(memory_space=pltpu.SMEM))
        args.append(sink)
    for arr, head_fn in (*q_parts, *k_parts):
        in_specs.append(pl.BlockSpec(blk, functools.partial(
            lambda b, h, head_fn: (head_fn(h), z0 + b, 0), head_fn=head_fn)))
        args.append(arr)
    v_arr, v_head = v_part
    if v_transposed:
        in_specs.append(pl.BlockSpec((1, LANE, ctx_len), lambda b, h: (v_head(h), 0, z0 + b)))
    else:
        in_specs.append(pl.BlockSpec(blk, lambda b, h: (v_head(h), z0 + b, 0)))
    args.append(v_arr)
    return pl.pallas_call(
        functools.partial(_ctx_kernel, two_part=len(q_parts) == 2, use_sink=sink is not None,
                          v_transposed=v_transposed),
        grid=(batch, n_heads),
        in_specs=in_specs,
        out_specs=pl.BlockSpec((ctx_len, LANE), lambda b, h: (b, h)),
        out_shape=jax.ShapeDtypeStruct((batch * ctx_len, n_heads * LANE), BF16),
        compiler_params=_cparams("parallel", "parallel"),
        name="ctx_attention",
    )(*args)


def _rope_angles(n_tokens, rot_dim):
    t = jnp.arange(n_tokens)
    row = (t // GRID_W).astype(F32)
    col = (t % GRID_W).astype(F32)
    n_freq = rot_dim // 4
    inv_freq = ROPE_BASE ** (-jnp.arange(n_freq, dtype=F32) / n_freq)
    ang = jnp.concatenate([row[:, None] * inv_freq, col[:, None] * inv_freq], axis=-1)
    return jnp.cos(ang), jnp.sin(ang)


def _with_identity_tile(tab, fill, tm):
    return jnp.concatenate([tab, jnp.broadcast_to(fill, (tm, LANE))], axis=0)


def swa_rope_tables(seq, scale, tm):
    cos, sin = _rope_angles(seq, HEAD_DIM)
    c = jnp.concatenate([cos, cos], axis=-1)
    s = jnp.concatenate([-sin, sin], axis=-1)
    one = jnp.ones((LANE,), F32)
    zero = jnp.zeros((LANE,), F32)
    ident_c = _with_identity_tile(jnp.ones_like(c), one, tm)
    ident_s = _with_identity_tile(jnp.zeros_like(s), zero, tm)
    c_sets = jnp.stack([_with_identity_tile(c * scale, one * scale, tm), _with_identity_tile(c, one, tm), ident_c])
    s_sets = jnp.stack([_with_identity_tile(s * scale, zero, tm), _with_identity_tile(s, zero, tm), ident_s])
    return c_sets, s_sets


def mla_rope_tables(seq, scale, tm):
    cos, sin = _rope_angles(seq, MLA_ROPE)
    half = MLA_ROPE // 2
    z = jnp.zeros((seq, half), F32)
    c = jnp.concatenate([cos, cos, z, z], axis=-1)
    s1 = jnp.concatenate([-sin, z, z, z], axis=-1)
    s2 = jnp.concatenate([z, sin, z, z], axis=-1)
    one = jnp.ones((LANE,), F32)
    zero = jnp.zeros((LANE,), F32)
    wt = _with_identity_tile
    c_sets = jnp.stack([wt(jnp.ones_like(c) * scale, one * scale, tm), wt(c * scale, one * scale, tm), wt(c, one, tm)])
    s1_sets = jnp.stack([wt(jnp.zeros_like(c), zero, tm), wt(s1 * scale, zero, tm), wt(s1, zero, tm)])
    s2_sets = jnp.stack([wt(jnp.zeros_like(c), zero, tm), wt(s2 * scale, zero, tm), wt(s2, zero, tm)])
    return c_sets, s1_sets, s2_sets


def kernel(x, c, ctx, c_ctx, w_mod, b_mod, norm_g, ffn_w_in, ffn_w_out, na_w_qkv, na_rpb, na_w_o,
           mla_w_down, mla_q_norm_g, mla_w_q_up, mla_kv_norm_g, mla_w_kv_up, mla_w_o,
           swa_w_qkv, swa_sink, swa_w_o, final_norm_g):
    batch, seq, d = x.shape
    ctx_len = ctx.shape[1]
    depth = w_mod.shape[0]
    tf_, tp, to_ = FFN_ROW_TILE, PROJ_ROW_TILE, OUT_ROW_TILE
    nx, nz = batch * seq, batch * ctx_len
    n_groups = batch + 1
    z0 = nx // ctx_len
    assert nx % ctx_len == 0 and all(seq % t == 0 and nz % t == 0 for t in (tf_, tp, to_))
    ffn_rows = dict(tiles_per_group=seq // tf_, n_groups=n_groups)
    proj_rows = dict(tiles_per_group=seq // tp, n_groups=n_groups, n_tiles=(nx + nz) // tp)
    pos_tiles = seq // tp

    n_cond = 8
    cond = jnp.concatenate([c, c_ctx[None, :], jnp.zeros((n_cond - batch - 1, d), F32)], axis=0)
    mods_all = mod_params(cond, w_mod, b_mod).reshape(depth, n_cond, N_MOD, d)

    qk_scale = HEAD_DIM ** -0.5 * LOG2E
    mla_scale = (MLA_NOPE + MLA_ROPE) ** -0.5 * LOG2E

    stream = None
    for li in range(depth):
        last = li == depth - 1
        j = li // N_MIXERS
        kind = li % N_MIXERS
        mods = mods_all[li]

        if li == 0:
            first_w = stacked_ffn_weights(ffn_w_in[0, 0].astype(BF16), ffn_w_out[0, 0].astype(BF16), (), FFN_CHUNK)
            all_tiles = (nx + nz) // to_
            stream = ffn_half_step(x.reshape(nx, d), norm_g[li, 0], mods, first_w, k0=0, n_tiles=all_tiles,
                                   total_tiles=all_tiles, tile_offset=0, tf=FFN_CHUNK, tm=to_,
                                   x_tail=ctx.reshape(nz, d), tiles_per_group=seq // to_, n_groups=n_groups)
        else:
            stream = ffn_half_step_f32_weights(stream, norm_g[li, 0], mods, ffn_w_in, ffn_w_out, (li, 0),
                                               k0=0, total_tiles=(nx + nz) // tf_, **ffn_rows)

        proj = dict(mods=mods, mod_idx=(3, 4), **proj_rows)
        if kind == 0:
            col_scale = jnp.concatenate([jnp.full((NA_HEADS * HEAD_DIM,), qk_scale, F32),
                                         jnp.ones((2 * NA_HEADS * HEAD_DIM,), F32)])
            qkv = norm_linear(stream, 0, d, norm_g[li, 1], na_w_qkv[j].astype(BF16),
                              col_scale=col_scale, tn=1024, **proj)
            t2 = na_bias_table(na_rpb[j])
            ox = na_attention(qkv, t2, batch=batch, seq=seq, ctx_len=ctx_len)
            if not last:
                hd = lambda part: (qkv, lambda h: part * NA_HEADS + h)
                oz = ctx_attention([hd(0)], [hd(1)], hd(2), batch=batch, ctx_len=ctx_len,
                                   z0=z0, n_heads=NA_HEADS)
            w_o = na_w_o[j]
        elif kind == 1:
            w_down = mla_w_down[j]
            lora = MLA_Q_LORA + MLA_KV_LORA
            cqkv = norm_linear(stream, 0, d, norm_g[li, 1], w_down[:, :lora].astype(BF16),
                               layout="rows", out_dtype=F32, tn=lora, **proj)
            c_t, s1_t, s2_t = mla_rope_tables(seq, mla_scale, tp)
            w_kpe = jnp.pad(w_down[:, lora:], ((0, 0), (0, LANE - MLA_ROPE))).astype(BF16)
            kp = norm_linear(stream, 0, d, norm_g[li, 1], w_kpe, rope=(96, 32),
                             tables=(c_t[2:], s1_t[2:], s2_t[2:]), pos_tiles=pos_tiles,
                             tn=LANE, **proj)
            wq = mla_w_q_up[j].reshape(MLA_Q_LORA, MLA_HEADS, MLA_NOPE + MLA_ROPE)
            wq_n = wq[:, :, :MLA_NOPE].reshape(MLA_Q_LORA, MLA_HEADS * MLA_NOPE).astype(BF16)
            wq_p = jnp.pad(wq[:, :, MLA_NOPE:], ((0, 0), (0, 0), (0, LANE - MLA_ROPE)))
            wq_p = wq_p.reshape(MLA_Q_LORA, MLA_HEADS * LANE).astype(BF16)
            qn = norm_linear(cqkv, 0, MLA_Q_LORA, mla_q_norm_g[j], wq_n,
                             col_scale=jnp.full((MLA_HEADS * MLA_NOPE,), mla_scale, F32), tn=1024, **proj_rows)
            qp = norm_linear(cqkv, 0, MLA_Q_LORA, mla_q_norm_g[j], wq_p, rope=(96, 32),
                             tables=(c_t[1:2], s1_t[1:2], s2_t[1:2]), pos_tiles=pos_tiles, tn=512, **proj_rows)
            wkv = mla_w_kv_up[j].reshape(MLA_KV_LORA, MLA_HEADS, MLA_NOPE + MLA_V)
            w_kn = wkv[:, :, :MLA_NOPE].reshape(MLA_KV_LORA, -1).astype(BF16)
            w_v = wkv[:, :, MLA_NOPE:].reshape(MLA_KV_LORA, -1).astype(BF16)
            kn = norm_linear(cqkv, 1, MLA_KV_LORA, mla_kv_norm_g[j], w_kn, tn=1024, **proj_rows)
            vt = norm_linear(cqkv, 1, MLA_KV_LORA, mla_kv_norm_g[j], w_v, layout="heads_t", tn=1024, **proj_rows)
            ox = mla_attention(qn, qp, kn, kp, vt, batch=batch, seq=seq, ctx_len=ctx_len)
            if not last:
                oz = ctx_attention([(qn, lambda h: h), (qp, lambda h: h)],
                                   [(kn, lambda h: h), (kp, lambda h: 0)], (vt, lambda h: h),
                                   batch=batch, ctx_len=ctx_len, z0=z0, n_heads=MLA_HEADS, v_transposed=True)
            w_o = mla_w_o[j]
        else:
            c_t, s_t = swa_rope_tables(seq, qk_scale, tp)
            n_q_blocks = SWA_HEADS * HEAD_DIM // 512
            qkv = norm_linear(stream, 0, d, norm_g[li, 1], swa_w_qkv[j].astype(BF16), rope=(64,),
                              tables=(c_t, s_t), set_thresholds=(n_q_blocks, n_q_blocks + 1),
                              pos_tiles=pos_tiles, tn=512, **proj)
            ox = swa_attention(qkv, swa_sink[j], batch=batch, seq=seq, ctx_len=ctx_len)
            if not last:
                g = SWA_HEADS // SWA_KV_HEADS
                oz = ctx_attention([(qkv, lambda h: h)], [(qkv, lambda h: SWA_HEADS + h // g)],
                                   (qkv, lambda h: SWA_HEADS + SWA_KV_HEADS + h // g),
                                   batch=batch, ctx_len=ctx_len, z0=z0, n_heads=SWA_HEADS,
                                   sink=swa_sink[j])
            w_o = swa_w_o[j]

        n_rows = nx if last else nx + nz
        stream = linear_residual(ox, w_o.astype(BF16), stream, mods, k_gate=5, n_tiles=n_rows // to_,
                                 a_tail=None if last else oz, tiles_per_group=seq // to_, n_groups=n_groups)
        stream = ffn_half_step_f32_weights(stream, norm_g[li, 2], mods, ffn_w_in, ffn_w_out, (li, 1), k0=6,
                                           total_tiles=n_rows // tf_,
                                           final_gain=final_norm_g if last else None, **ffn_rows)

    return stream.reshape(batch, seq, d)
```

```python
import functools
import math

import jax
import jax.numpy as jnp
from jax import lax
from jax.experimental import pallas as pl
from jax.experimental.pallas import tpu as pltpu

DEPTH = 4
GRID_W = 64
N_MIXERS = 3
N_MOD = 9
RMS_EPS = 1e-6
ROPE_BASE = 10000.0
NEG_INF = -1e30
LOG2E = math.log2(math.e)

NA_HEADS = 16
NA_KH = 8
NA_KW = 16
HEAD_DIM = 128

MLA_HEADS = 16
MLA_Q_LORA = 512
MLA_KV_LORA = 512
MLA_NOPE = 128
MLA_ROPE = 64
MLA_V = 128

SWA_HEADS = 16
SWA_KV_HEADS = 4
SWA_WINDOW = 128

LANE = 128
FFN_ROW_TILE = 1024
FFN_VMEM_LIMIT = 62 * 1024 * 1024
OUT_ROW_TILE = 512
PROJ_ROW_TILE = 1024
FFN_CHUNK = 512
FFN_CAST_CHUNK = 256
FFN_NORM_SLABS = 4
PROJ_NORM_SLABS = 4
VMEM_LIMIT = 56 * 1024 * 1024

F32 = jnp.float32
BF16 = jnp.bfloat16


def _cparams(*sem):
    return pltpu.CompilerParams(dimension_semantics=sem, vmem_limit_bytes=VMEM_LIMIT)


def _dot(a, b):
    return jnp.dot(a, b, preferred_element_type=F32)


def _dot_t(a, b):
    return lax.dot_general(a, b, (((1,), (1,)), ((), ())), preferred_element_type=F32)


def _sigmoid(x):
    return 1.0 / (1.0 + jnp.exp(-x))


def _rmsnorm_rows(x, g):
    ms = jnp.mean(x * x, axis=-1, keepdims=True)
    return x * lax.rsqrt(ms + RMS_EPS) * g


def _modulated_norm(read_x, g_ref, mod_ref, k_shift, k_scale):
    x = read_x()
    r = lax.rsqrt(jnp.mean(x * x, axis=-1, keepdims=True) + RMS_EPS)
    gs = g_ref[...] * (1.0 + mod_ref[0, k_scale:k_scale + 1, :])
    return (read_x() * r) * gs + mod_ref[0, k_shift:k_shift + 1, :]


def _group_fn(tiles_per_group, n_groups):
    return lambda i: jnp.minimum(i // tiles_per_group, n_groups - 1)


def _two_source_specs(block, split_tiles):
    return [pl.BlockSpec(block, lambda i, j: (jnp.minimum(i, split_tiles - 1), 0)),
            pl.BlockSpec(block, lambda i, j: (jnp.maximum(i - split_tiles, 0), 0))]


def _read_rows(ref, tail_ref, split_tiles):
    if tail_ref is None:
        return ref[...]
    return jnp.where(pl.program_id(0) < split_tiles, ref[...], tail_ref[...])


def _mod_kernel(c_ref, w_ref, b_ref, o_ref):
    c = c_ref[...]
    a = (c * _sigmoid(c)).astype(BF16)
    o_ref[0] = _dot(a, w_ref[0].astype(BF16)) + b_ref[0]


def mod_params(cond, w_mod, b_mod):
    depth, d, n = w_mod.shape
    g = cond.shape[0]
    tn = math.gcd(n, 1024)
    return pl.pallas_call(
        _mod_kernel,
        grid=(depth, n // tn),
        in_specs=[
            pl.BlockSpec((g, d), lambda l, j: (0, 0)),
            pl.BlockSpec((1, d, tn), lambda l, j: (l, 0, j)),
            pl.BlockSpec((1, 1, tn), lambda l, j: (l, 0, j)),
        ],
        out_specs=pl.BlockSpec((1, g, tn), lambda l, j: (l, 0, j)),
        out_shape=jax.ShapeDtypeStruct((depth, g, n), F32),
        compiler_params=_cparams("parallel", "parallel"),
        name="mod_params",
    )(cond, w_mod, b_mod.reshape(depth, 1, n))


def _ffn_kernel(*refs, k0, final_norm, cast_weights, n_steps):
    it = iter(refs)
    x_ref, g_ref, mod_ref, wg_ref, wu_ref, wo_ref = (next(it) for _ in range(6))
    gf_ref = next(it) if final_norm else None
    o_ref = next(it)
    w_bf_refs = [next(it) for _ in range(3)] if cast_weights else None
    h_ref = next(it)
    j = pl.program_id(1)

    def weights():
        wg, wu, wo = wg_ref[...], wu_ref[...], wo_ref[...]
        if cast_weights:
            wg, wu, wo = wg.astype(BF16), wu.astype(BF16), wo.astype(BF16)
            for ref, w in zip(w_bf_refs, (wg, wu, wo)):
                ref[...] = w
        return wg, wu, wo

    def swiglu_down(h, wg, wu, wo):
        gate = _dot(h, wg)
        up = _dot(h, wu)
        return _dot((gate * _sigmoid(gate) * up).astype(BF16), wo)

    slab = o_ref.shape[0] // FFN_NORM_SLABS
    slabs = [slice(q * slab, (q + 1) * slab) for q in range(FFN_NORM_SLABS)]

    @pl.when(j == 0)
    def _():
        wg, wu, wo = weights()
        for rows in slabs:
            h = _modulated_norm(lambda rows=rows: x_ref[rows, :], g_ref, mod_ref, k0, k0 + 1).astype(BF16)
            h_ref[rows, :] = h
            o_ref[rows, :] = swiglu_down(h, wg, wu, wo)

    @pl.when((j > 0) & (j < n_steps - 1))
    def _():
        o_ref[...] += swiglu_down(h_ref[...], *weights())

    @pl.when(j == n_steps - 1)
    def _():
        wg, wu, wo = weights()
        for rows in slabs:
            acc = o_ref[rows, :] + swiglu_down(h_ref[rows, :], wg, wu, wo)
            y = x_ref[rows, :] + (0.5 * mod_ref[0, k0 + 2:k0 + 3, :]) * acc
            if final_norm:
                y = _rmsnorm_rows(y, gf_ref[...])
            o_ref[rows, :] = y


def stacked_ffn_weights(w_in, w_out, lead, tf):
    d, f = w_in.shape[-2], w_out.shape[-2]
    nf = f // tf
    sq = (None,) * len(lead)
    return ((w_in, sq + (d, tf), lambda i, j: lead + (0, j)),
            (w_in, sq + (d, tf), lambda i, j: lead + (0, j + nf)),
            (w_out, sq + (tf, d), lambda i, j: lead + (j, 0)))


def plain_ffn_weights(w_gate, w_up, w_down, tf):
    d = w_gate.shape[0]
    return ((w_gate, (d, tf), lambda i, j: (0, j)),
            (w_up, (d, tf), lambda i, j: (0, j)),
            (w_down, (tf, d), lambda i, j: (j, 0)))


def ffn_half_step(x, gain, mods, weights, *, k0, n_tiles, tile_offset, tiles_per_group, n_groups, tf,
                  final_gain=None, cast_weights=False, tm=FFN_ROW_TILE):
    d = x.shape[1]
    f = weights[2][0].shape[-2]
    nf = f // tf
    assert nf >= 2
    grp = _group_fn(tiles_per_group, n_groups)
    x_mode = dict(pipeline_mode=pl.Buffered(1)) if n_tiles == 1 else {}
    in_specs = [pl.BlockSpec((tm, d), lambda i, j: (i + tile_offset, 0), **x_mode),
                pl.BlockSpec((1, d), lambda i, j: (0, 0)),
                pl.BlockSpec((1, N_MOD, d), lambda i, j: (grp(i + tile_offset), 0, 0))]
    args = [x, gain.reshape(1, d), mods]
    for arr, block, index_map in weights:
        in_specs.append(pl.BlockSpec(block, index_map))
        args.append(arr)
    if final_gain is not None:
        in_specs.append(pl.BlockSpec((1, d), lambda i, j: (0, 0)))
        args.append(final_gain.reshape(1, d))
    out_specs = [pl.BlockSpec((tm, d), lambda i, j: (i + tile_offset, 0))]
    out_shape = [jax.ShapeDtypeStruct(x.shape, F32)]
    if cast_weights:
        out_specs += [pl.BlockSpec((d, tf), lambda i, j: (0, j)), pl.BlockSpec((d, tf), lambda i, j: (0, j)),
                      pl.BlockSpec((tf, d), lambda i, j: (j, 0))]
        out_shape += [jax.ShapeDtypeStruct((d, f), BF16), jax.ShapeDtypeStruct((d, f), BF16),
                      jax.ShapeDtypeStruct((f, d), BF16)]
    out = pl.pallas_call(
        functools.partial(_ffn_kernel, k0=k0, final_norm=final_gain is not None,
                          cast_weights=cast_weights, n_steps=nf),
        grid=(n_tiles, nf),
        in_specs=in_specs,
        out_specs=out_specs,
        out_shape=out_shape,
        input_output_aliases={0: 0},
        scratch_shapes=[pltpu.VMEM((tm, d), BF16)],
        compiler_params=pltpu.CompilerParams(dimension_semantics=("parallel", "arbitrary"),
                                             vmem_limit_bytes=FFN_VMEM_LIMIT),
        name="ffn_half_step",
    )(*args)
    return out if cast_weights else out[0]


def ffn_half_step_f32_weights(x, gain, mods, w_in, w_out, lead, *, total_tiles, **kw):
    y, w_gate, w_up, w_down = ffn_half_step(
        x, gain, mods, stacked_ffn_weights(w_in, w_out, lead, FFN_CAST_CHUNK), n_tiles=1,
        tile_offset=0, tf=FFN_CAST_CHUNK, cast_weights=True, **kw)
    return ffn_half_step(
        y, gain, mods, plain_ffn_weights(w_gate, w_up, w_down, FFN_CHUNK), n_tiles=total_tiles - 1,
        tile_offset=1, tf=FFN_CHUNK, **kw)


def _norm_linear_kernel(*refs, mod_idx, rope, layout, n_chunks):
    it = iter(refs)
    x_ref, g_ref = next(it), next(it)
    mod_ref = next(it) if mod_idx is not None else None
    w_ref = next(it)
    if rope is None:
        cs_ref = next(it)
    else:
        tab_refs = [next(it) for _ in range(len(rope) + 1)]
    o_ref, h_ref = next(it), next(it)

    def project(h, rows):
        acc = _dot(h, w_ref[...])
        if rope is None:
            acc = acc * cs_ref[...]
        for c in range(n_chunks):
            a = acc[:, c * LANE:(c + 1) * LANE]
            if rope is not None:
                out = a * tab_refs[0][0, rows, :]
                for shift, tab in zip(rope, tab_refs[1:]):
                    out = out + pltpu.roll(a, shift, 1) * tab[0, rows, :]
                a = out
            if layout == "heads":
                o_ref[c, rows, :] = a.astype(o_ref.dtype)
            elif layout == "heads_t":
                o_ref[c, :, rows] = a.T.astype(o_ref.dtype)
            else:
                o_ref[rows, c * LANE:(c + 1) * LANE] = a.astype(o_ref.dtype)

    @pl.when(pl.program_id(1) == 0)
    def _():
        slab = x_ref.shape[0] // PROJ_NORM_SLABS
        for q in range(PROJ_NORM_SLABS):
            rows = slice(q * slab, (q + 1) * slab)
            if mod_idx is None:
                y = _rmsnorm_rows(x_ref[rows, :], g_ref[...])
            else:
                y = _modulated_norm(lambda rows=rows: x_ref[rows, :], g_ref, mod_ref, *mod_idx)
            h = y.astype(BF16)
            h_ref[rows, :] = h
            project(h, rows)

    @pl.when(pl.program_id(1) > 0)
    def _():
        project(h_ref[...], slice(None))


def norm_linear(x, x_col_block, k, gain, w, *, n_tiles, tiles_per_group, n_groups,
                mods=None, mod_idx=None, col_scale=None, rope=None, tables=None,
                set_thresholds=(), pos_tiles=None, layout="heads", out_dtype=BF16,
                tm=PROJ_ROW_TILE, tn=512):
    n = w.shape[1]
    nj = n // tn
    n_chunks = tn // LANE
    grp = _group_fn(tiles_per_group, n_groups)
    in_specs = [
        pl.BlockSpec((tm, k), lambda i, j: (i, x_col_block)),
        pl.BlockSpec((1, k), lambda i, j: (0, 0)),
    ]
    args = [x, gain.reshape(1, k)]
    if mod_idx is not None:
        in_specs.append(pl.BlockSpec((1, N_MOD, k), lambda i, j: (grp(i), 0, 0)))
        args.append(mods)
    in_specs.append(pl.BlockSpec((k, tn), lambda i, j: (0, j)))
    args.append(w)
    if rope is None:
        if col_scale is None:
            col_scale = jnp.ones((n,), F32)
        in_specs.append(pl.BlockSpec((1, tn), lambda i, j: (0, j)))
        args.append(col_scale.reshape(1, n))
    else:
        n_lat_tiles = tiles_per_group * (n_groups - 1)

        def tab_map(i, j):
            s = sum((j >= t).astype(jnp.int32) for t in set_thresholds) if set_thresholds else 0
            p = jnp.where(i < n_lat_tiles, i % pos_tiles, pos_tiles)
            return (s, p, 0)

        for t in tables:
            in_specs.append(pl.BlockSpec((1, tm, LANE), tab_map))
            args.append(t)
    rows = n_tiles * tm
    if layout == "heads":
        out_spec = pl.BlockSpec((n_chunks, tm, LANE), lambda i, j: (j, i, 0))
        out_shape = jax.ShapeDtypeStruct((n // LANE, rows, LANE), out_dtype)
    elif layout == "heads_t":
        out_spec = pl.BlockSpec((n_chunks, LANE, tm), lambda i, j: (j, 0, i))
        out_shape = jax.ShapeDtypeStruct((n // LANE, LANE, rows), out_dtype)
    else:
        out_spec = pl.BlockSpec((tm, tn), lambda i, j: (i, j))
        out_shape = jax.ShapeDtypeStruct((rows, n), out_dtype)
    return pl.pallas_call(
        functools.partial(_norm_linear_kernel, mod_idx=mod_idx, rope=rope,
                          layout=layout, n_chunks=n_chunks),
        grid=(n_tiles, nj),
        in_specs=in_specs,
        out_specs=out_spec,
        out_shape=out_shape,
        scratch_shapes=[pltpu.VMEM((tm, k), BF16)],
        compiler_params=_cparams("parallel", "arbitrary"),
        name="norm_linear",
    )(*args)


def _linear_residual_kernel(*refs, k_gate, split_tiles):
    it = iter(refs)
    a_ref = next(it)
    at_ref = next(it) if split_tiles is not None else None
    w_ref, x_ref, mod_ref, o_ref = next(it), next(it), next(it), next(it)
    acc = _dot(_read_rows(a_ref, at_ref, split_tiles), w_ref[...])
    o_ref[...] = x_ref[...] + mod_ref[0, k_gate:k_gate + 1, :] * acc


def linear_residual(a, w, x, mods, *, k_gate, n_tiles, tiles_per_group, n_groups, a_tail=None,
                    tm=OUT_ROW_TILE):
    kdim, n = w.shape
    grp = _group_fn(tiles_per_group, n_groups)
    split_tiles = None if a_tail is None else a.shape[0] // tm
    if a_tail is None:
        in_specs, args = [pl.BlockSpec((tm, kdim), lambda i, j: (i, 0))], [a]
    else:
        in_specs, args = _two_source_specs((tm, kdim), split_tiles), [a, a_tail]
    in_specs += [
        pl.BlockSpec((kdim, n), lambda i, j: (0, 0)),
        pl.BlockSpec((tm, n), lambda i, j: (i, 0)),
        pl.BlockSpec((1, N_MOD, n), lambda i, j: (grp(i), 0, 0)),
    ]
    args += [w, x, mods]
    return pl.pallas_call(
        functools.partial(_linear_residual_kernel, k_gate=k_gate, split_tiles=split_tiles),
        grid=(n_tiles, 1),
        in_specs=in_specs,
        out_specs=pl.BlockSpec((tm, n), lambda i, j: (i, 0)),
        out_shape=jax.ShapeDtypeStruct((n_tiles * tm, n), F32),
        compiler_params=_cparams("parallel", "arbitrary"),
        name="linear_residual",
    )(*args)


NA_QROWS = 4
NA_QTOK = NA_QROWS * GRID_W
NA_BAND_BLOCKS = 3
NA_PAIRS = 2 * NA_KH


def _na_bias_kernel(rpb_ref, o_ref):
    h = pl.program_id(0)
    cq = lax.broadcasted_iota(jnp.int32, (GRID_W, 2 * GRID_W), 0)
    lane = lax.broadcasted_iota(jnp.int32, (GRID_W, 2 * GRID_W), 1)
    second = lane >= GRID_W
    ck = jnp.where(second, lane - GRID_W, lane)
    n_dc = 2 * NA_KW - 1
    code = ck - cq + (NA_KW - 1) + jnp.where(second, n_dc, 0)
    start = jnp.clip(cq - NA_KW // 2, 0, GRID_W - NA_KW)
    in_win = (ck >= start) & (ck < start + NA_KW)
    for p in range(NA_PAIRS):
        tile = jnp.zeros((GRID_W, 2 * GRID_W), F32)
        for half in range(2):
            dr = p - NA_KH + half
            if not (-(NA_KH - 1) <= dr <= NA_KH - 1):
                continue
            for dc in range(n_dc):
                val = rpb_ref[h, dr + NA_KH - 1, dc]
                tile = jnp.where(code == half * n_dc + dc, val, tile)
        o_ref[0, p] = jnp.where(in_win, tile * LOG2E, NEG_INF)


def na_bias_table(rpb):
    heads = rpb.shape[0]
    return pl.pallas_call(
        _na_bias_kernel,
        grid=(heads,),
        in_specs=[pl.BlockSpec(memory_space=pltpu.SMEM)],
        out_specs=pl.BlockSpec((1, NA_PAIRS, GRID_W, 2 * GRID_W), lambda h: (h, 0, 0, 0)),
        out_shape=jax.ShapeDtypeStruct((heads, NA_PAIRS, GRID_W, 2 * GRID_W), F32),
        compiler_params=_cparams("parallel"),
        name="na_bias_table",
    )(rpb)


def _na_kernel(q_ref, k0_ref, k1_ref, k2_ref, v0_ref, v1_ref, v2_ref, kz_ref, vz_ref,
               t2_ref, o_ref, vm_ref, ob_ref, *, rows):
    qb = pl.program_id(1)
    r0 = NA_QROWS * qb
    u0 = jnp.clip(r0 - NA_KH // 2, 0, rows - NA_BAND_BLOCKS * NA_QROWS)
    k_refs = (k0_ref, k1_ref, k2_ref, kz_ref)
    v_refs = (v0_ref, v1_ref, v2_ref, vz_ref)

    q_row = r0 + (lax.broadcasted_iota(jnp.int32, (NA_QTOK, NA_QTOK), 0) >> 6)
    k_off = lax.broadcasted_iota(jnp.int32, (NA_QTOK, NA_QTOK), 1) >> 6
    rs = jnp.clip(q_row - NA_KH // 2, 0, rows - NA_KH)
    for i in range(NA_BAND_BLOCKS):
        rk = u0 + NA_QROWS * i + k_off
        vm_ref[i] = jnp.where((rk >= rs) & (rk < rs + NA_KH), 0.0, NEG_INF)

    def scores(h):
        return _dot_t(q_ref[h], jnp.concatenate([r[h] for r in k_refs], axis=0))

    def finish_head(h, s_all):
        s = []
        for i in range(NA_BAND_BLOCKS):
            bias_rows = []
            for qi in range(NA_QROWS):
                pair = []
                for c in range(2):
                    dr0 = u0 + NA_QROWS * i + 2 * c - (r0 + qi)
                    pair.append(t2_ref[h, jnp.clip(dr0 + NA_KH, 0, NA_PAIRS - 1)])
                bias_rows.append(jnp.concatenate(pair, axis=1))
            s.append(s_all[:, i * NA_QTOK:(i + 1) * NA_QTOK] + jnp.concatenate(bias_rows, axis=0) + vm_ref[i])
        s.append(s_all[:, NA_BAND_BLOCKS * NA_QTOK:])
        m = jnp.maximum(jnp.maximum(s[0], s[1]), jnp.maximum(s[2], s[3])).max(axis=-1, keepdims=True)
        p = jnp.concatenate([jnp.exp2(si - m) for si in s], axis=1).astype(BF16)
        v = jnp.concatenate([r[h] for r in v_refs], axis=0)
        o = _dot(p, jnp.concatenate([v, jnp.ones_like(v)], axis=1))
        ob_ref[h] = (o[:, :HEAD_DIM] * (1.0 / o[:, HEAD_DIM:])).astype(BF16)

    s_next = scores(0)
    for h in range(NA_HEADS):
        s_cur = s_next
        if h + 1 < NA_HEADS:
            s_next = scores(h + 1)
        finish_head(h, s_cur)
    for h in range(NA_HEADS):
        o_ref[:, h * HEAD_DIM:(h + 1) * HEAD_DIM] = ob_ref[h]


def na_attention(qkv, t2, *, batch, seq, ctx_len):
    rows = seq // GRID_W
    nqb = seq // NA_QTOK
    z0 = batch * seq // ctx_len
    assert ctx_len == NA_QTOK and rows >= NA_BAND_BLOCKS * NA_QROWS
    blk = (NA_HEADS, NA_QTOK, HEAD_DIM)

    def band(i):
        return lambda b, q: (0, b * nqb + jnp.clip(q - 1, 0, nqb - NA_BAND_BLOCKS) + i, 0)

    def shifted(fn, part):
        return lambda b, q: (part,) + fn(b, q)[1:]

    in_specs = [pl.BlockSpec(blk, lambda b, q: (0, b * nqb + q, 0))]
    in_specs += [pl.BlockSpec(blk, shifted(band(i), 1)) for i in range(NA_BAND_BLOCKS)]
    in_specs += [pl.BlockSpec(blk, shifted(band(i), 2)) for i in range(NA_BAND_BLOCKS)]
    in_specs += [pl.BlockSpec(blk, lambda b, q: (1, z0 + b, 0)),
                 pl.BlockSpec(blk, lambda b, q: (2, z0 + b, 0)),
                 pl.BlockSpec(t2.shape, lambda b, q: (0, 0, 0, 0))]
    return pl.pallas_call(
        functools.partial(_na_kernel, rows=rows),
        grid=(batch, nqb),
        in_specs=in_specs,
        out_specs=pl.BlockSpec((NA_QTOK, NA_HEADS * HEAD_DIM), lambda b, q: (b * nqb + q, 0)),
        out_shape=jax.ShapeDtypeStruct((batch * seq, NA_HEADS * HEAD_DIM), BF16),
        scratch_shapes=[pltpu.VMEM((NA_BAND_BLOCKS, NA_QTOK, NA_QTOK), F32),
                        pltpu.VMEM(blk, BF16)],
        compiler_params=_cparams("parallel", "arbitrary"),
        name="na_attention",
    )(*([qkv] * 9), t2)


MLA_TQ = 1024
MLA_TK = 1024


def _mla_kernel(qn_ref, qp_ref, kn_ref, kp_ref, vt_ref, knz_ref, kpz_ref, vtz_ref, o_ref, *, seq):
    q = jnp.concatenate([qn_ref[0], qp_ref[0]], axis=1)
    tq = q.shape[0]

    n_blocks = seq // MLA_TK

    def scores(kb):
        if kb == n_blocks:
            return _dot_t(jnp.concatenate([knz_ref[0], kpz_ref[0]], axis=1), q)
        rows = slice(kb * MLA_TK, (kb + 1) * MLA_TK)
        return _dot_t(jnp.concatenate([kn_ref[0, rows, :], kp_ref[0, rows, :]], axis=1), q)

    m = jnp.full((1, tq), NEG_INF, F32)
    l = jnp.zeros((1, tq), F32)
    acc = jnp.zeros((MLA_V, tq), F32)
    s_next = scores(0)
    for kb in range(n_blocks + 1):
        s = s_next
        if kb < n_blocks:
            s_next = scores(kb + 1)
            vt = vt_ref[0, :, kb * MLA_TK:(kb + 1) * MLA_TK]
        else:
            vt = vtz_ref[0]
        m_new = jnp.maximum(m, s.max(axis=0, keepdims=True))
        alpha = jnp.exp2(m - m_new)
        p = jnp.exp2(s - m_new)
        l = alpha * l + p.sum(axis=0, keepdims=True)
        acc = alpha * acc + _dot(vt, p.astype(BF16))
        m = m_new
    o_ref[...] = (acc * (1.0 / l)).T.astype(BF16)


def mla_attention(qn, qp, kn, kp, vt, *, batch, seq, ctx_len):
    h_ = MLA_HEADS
    nqb = seq // MLA_TQ
    z0 = batch * seq // ctx_len
    qblk = (1, MLA_TQ, LANE)
    kblk = (1, seq, LANE)
    zblk = (1, ctx_len, LANE)
    in_specs = [
        pl.BlockSpec(qblk, lambda b, h, q: (h, b * nqb + q, 0)),
        pl.BlockSpec(qblk, lambda b, h, q: (h, b * nqb + q, 0)),
        pl.BlockSpec(kblk, lambda b, h, q: (h, b, 0)),
        pl.BlockSpec(kblk, lambda b, h, q: (0, b, 0)),
        pl.BlockSpec((1, MLA_V, seq), lambda b, h, q: (h, 0, b)),
        pl.BlockSpec(zblk, lambda b, h, q: (h, z0 + b, 0)),
        pl.BlockSpec(zblk, lambda b, h, q: (0, z0 + b, 0)),
        pl.BlockSpec((1, MLA_V, ctx_len), lambda b, h, q: (h, 0, z0 + b)),
    ]
    return pl.pallas_call(
        functools.partial(_mla_kernel, seq=seq),
        grid=(batch, h_, nqb),
        in_specs=in_specs,
        out_specs=pl.BlockSpec((MLA_TQ, MLA_V), lambda b, h, q: (b * nqb + q, h)),
        out_shape=jax.ShapeDtypeStruct((batch * seq, h_ * MLA_V), BF16),
        compiler_params=_cparams("parallel", "parallel", "arbitrary"),
        name="mla_attention",
    )(qn, qp, kn, kp, vt, kn, kp, vt)


SWA_TQ = SWA_WINDOW


def _swa_kernel(sink_ref, q_ref, k0_ref, k1_ref, k2_ref, v0_ref, v1_ref, v2_ref,
                kz_ref, vz_ref, o_ref, *, n_blocks):
    qb = pl.program_id(1)
    g = SWA_HEADS // SWA_KV_HEADS
    m_rows = g * SWA_TQ
    row = lax.broadcasted_iota(jnp.int32, (m_rows, SWA_TQ), 0) & (SWA_TQ - 1)
    lane = lax.broadcasted_iota(jnp.int32, (m_rows, SWA_TQ), 1)
    mask_prev = (lane >= row) & (qb >= 1)
    mask_next = (lane <= row) & (qb <= n_blocks - 2)
    grp_id = lax.broadcasted_iota(jnp.int32, (m_rows, 1), 0) >> 7
    k_refs = (k0_ref, k1_ref, k2_ref, kz_ref)
    v_refs = (v0_ref, v1_ref, v2_ref, vz_ref)

    def scores(kvh):
        q4 = q_ref[kvh * g:(kvh + 1) * g].reshape(m_rows, HEAD_DIM)
        return _dot_t(q4, jnp.concatenate([r[kvh] for r in k_refs], axis=0))

    s_next = scores(0)
    for kvh in range(SWA_KV_HEADS):
        s_all = s_next
        if kvh + 1 < SWA_KV_HEADS:
            s_next = scores(kvh + 1)
        sink = jnp.zeros((m_rows, 1), F32)
        for gi in range(g):
            sink = jnp.where(grp_id == gi, sink_ref[kvh * g + gi] * LOG2E, sink)
        s = [jnp.where(mask_prev, s_all[:, :SWA_TQ], NEG_INF),
             s_all[:, SWA_TQ:2 * SWA_TQ],
             jnp.where(mask_next, s_all[:, 2 * SWA_TQ:3 * SWA_TQ], NEG_INF)]
        s += [s_all[:, c:c + SWA_TQ] for c in range(3 * SWA_TQ, s_all.shape[1], SWA_TQ)]
        mx = s[0]
        for si in s[1:]:
            mx = jnp.maximum(mx, si)
        m = jnp.maximum(mx.max(axis=-1, keepdims=True), sink)
        p = [jnp.exp2(si - m) for si in s]
        ps = p[0]
        for pi in p[1:]:
            ps = ps + pi
        l = ps.sum(axis=-1, keepdims=True) + jnp.exp2(sink - m)
        o = _dot(jnp.concatenate(p, axis=1).astype(BF16), jnp.concatenate([r[kvh] for r in v_refs], axis=0))
        o = (o * (1.0 / l)).astype(BF16)
        for gi in range(g):
            hq = kvh * g + gi
            o_ref[:, hq * HEAD_DIM:(hq + 1) * HEAD_DIM] = o[gi * SWA_TQ:(gi + 1) * SWA_TQ]


def swa_attention(qkv, sink, *, batch, seq, ctx_len):
    nb = seq // SWA_TQ
    kvh = SWA_KV_HEADS
    qpart = SWA_HEADS // kvh
    z0 = batch * seq // ctx_len
    kblk = (kvh, SWA_TQ, HEAD_DIM)
    zblk = (kvh, ctx_len, HEAD_DIM)
    assert ctx_len % SWA_TQ == 0

    def kmap(part, d):
        return lambda b, q: (part, b * nb + jnp.clip(q + d, 0, nb - 1), 0)

    in_specs = [pl.BlockSpec(memory_space=pltpu.SMEM),
                pl.BlockSpec((SWA_HEADS, SWA_TQ, HEAD_DIM), lambda b, q: (0, b * nb + q, 0))]
    in_specs += [pl.BlockSpec(kblk, kmap(qpart, d)) for d in (-1, 0, 1)]
    in_specs += [pl.BlockSpec(kblk, kmap(qpart + 1, d)) for d in (-1, 0, 1)]
    in_specs += [pl.BlockSpec(zblk, lambda b, q: (qpart, z0 + b, 0)),
                 pl.BlockSpec(zblk, lambda b, q: (qpart + 1, z0 + b, 0))]
    return pl.pallas_call(
        functools.partial(_swa_kernel, n_blocks=nb),
        grid=(batch, nb),
        in_specs=in_specs,
        out_specs=pl.BlockSpec((SWA_TQ, SWA_HEADS * HEAD_DIM), lambda b, q: (b * nb + q, 0)),
        out_shape=jax.ShapeDtypeStruct((batch * seq, SWA_HEADS * HEAD_DIM), BF16),
        compiler_params=_cparams("parallel", "arbitrary"),
        name="swa_attention",
    )(sink, *([qkv] * 9))


def _ctx_kernel(*refs, n_q_parts, n_k_parts, n_heads, use_sink, v_transposed):
    it = iter(refs)
    sink_ref = next(it) if use_sink else None
    q_refs = [next(it) for _ in range(n_q_parts)]
    k_refs = [next(it) for _ in range(n_k_parts)]
    v_ref, o_ref = next(it), next(it)

    def head_of(ref, h):
        return ref[h * ref.shape[0] // n_heads]

    for h in range(n_heads):
        q = jnp.concatenate([head_of(r, h) for r in q_refs], axis=1)
        k = jnp.concatenate([head_of(r, h) for r in k_refs], axis=1)
        s = _dot_t(q, k)
        m = s.max(axis=-1, keepdims=True)
        if use_sink:
            sink = sink_ref[h] * LOG2E
            m = jnp.maximum(m, sink)
        p = jnp.exp2(s - m)
        l = p.sum(axis=-1, keepdims=True)
        if use_sink:
            l = l + jnp.exp2(sink - m)
        v = head_of(v_ref, h)
        pv = _dot_t(p.astype(BF16), v) if v_transposed else _dot(p.astype(BF16), v)
        o_ref[:, h * LANE:(h + 1) * LANE] = (pv * (1.0 / l)).astype(BF16)


def ctx_attention(q_parts, k_parts, v_part, *, batch, ctx_len, z0, n_heads, sink=None,
                  v_transposed=False):
    in_specs, args = [], []
    if sink is not None:
        in_specs.append(pl.BlockSpec(memory_space=pltpu.SMEM))
        args.append(sink)
    for arr, first, n in (*q_parts, *k_parts):
        in_specs.append(pl.BlockSpec((n, ctx_len, LANE), functools.partial(
            lambda b, blk: (blk, z0 + b, 0), blk=first // n)))
        args.append(arr)
    v_arr, v_first, v_n = v_part
    if v_transposed:
        in_specs.append(pl.BlockSpec((v_n, LANE, ctx_len), lambda b: (v_first // v_n, 0, z0 + b)))
    else:
        in_specs.append(pl.BlockSpec((v_n, ctx_len, LANE), lambda b: (v_first // v_n, z0 + b, 0)))
    args.append(v_arr)
    return pl.pallas_call(
        functools.partial(_ctx_kernel, n_q_parts=len(q_parts), n_k_parts=len(k_parts), n_heads=n_heads,
                          use_sink=sink is not None, v_transposed=v_transposed),
        grid=(batch,),
        in_specs=in_specs,
        out_specs=pl.BlockSpec((ctx_len, n_heads * LANE), lambda b: (b, 0)),
        out_shape=jax.ShapeDtypeStruct((batch * ctx_len, n_heads * LANE), BF16),
        compiler_params=_cparams("parallel"),
        name="ctx_attention",
    )(*args)


def _rope_angles(n_tokens, rot_dim):
    t = jnp.arange(n_tokens)
    row = (t // GRID_W).astype(F32)
    col = (t % GRID_W).astype(F32)
    n_freq = rot_dim // 4
    inv_freq = ROPE_BASE ** (-jnp.arange(n_freq, dtype=F32) / n_freq)
    ang = jnp.concatenate([row[:, None] * inv_freq, col[:, None] * inv_freq], axis=-1)
    return jnp.cos(ang), jnp.sin(ang)


def _with_identity_tile(tab, fill, tm):
    return jnp.concatenate([tab, jnp.broadcast_to(fill, (tm, LANE))], axis=0)


def swa_rope_tables(seq, scale, tm):
    cos, sin = _rope_angles(seq, HEAD_DIM)
    c = jnp.concatenate([cos, cos], axis=-1)
    s = jnp.concatenate([-sin, sin], axis=-1)
    one = jnp.ones((LANE,), F32)
    zero = jnp.zeros((LANE,), F32)
    ident_c = _with_identity_tile(jnp.ones_like(c), one, tm)
    ident_s = _with_identity_tile(jnp.zeros_like(s), zero, tm)
    c_sets = jnp.stack([_with_identity_tile(c * scale, one * scale, tm), _with_identity_tile(c, one, tm), ident_c])
    s_sets = jnp.stack([_with_identity_tile(s * scale, zero, tm), _with_identity_tile(s, zero, tm), ident_s])
    return c_sets, s_sets


def mla_rope_tables(seq, scale, tm):
    cos, sin = _rope_angles(seq, MLA_ROPE)
    half = MLA_ROPE // 2
    z = jnp.zeros((seq, half), F32)
    c = jnp.concatenate([cos, cos, z, z], axis=-1)
    s1 = jnp.concatenate([-sin, z, z, z], axis=-1)
    s2 = jnp.concatenate([z, sin, z, z], axis=-1)
    one = jnp.ones((LANE,), F32)
    zero = jnp.zeros((LANE,), F32)
    wt = _with_identity_tile
    c_sets = jnp.stack([wt(jnp.ones_like(c) * scale, one * scale, tm), wt(c * scale, one * scale, tm), wt(c, one, tm)])
    s1_sets = jnp.stack([wt(jnp.zeros_like(c), zero, tm), wt(s1 * scale, zero, tm), wt(s1, zero, tm)])
    s2_sets = jnp.stack([wt(jnp.zeros_like(c), zero, tm), wt(s2 * scale, zero, tm), wt(s2, zero, tm)])
    return c_sets, s1_sets, s2_sets


def kernel(x, c, ctx, c_ctx, w_mod, b_mod, norm_g, ffn_w_in, ffn_w_out, na_w_qkv, na_rpb, na_w_o,
           mla_w_down, mla_q_norm_g, mla_w_q_up, mla_kv_norm_g, mla_w_kv_up, mla_w_o,
           swa_w_qkv, swa_sink, swa_w_o, final_norm_g):
    batch, seq, d = x.shape
    ctx_len = ctx.shape[1]
    depth = w_mod.shape[0]
    tf_, tp, to_ = FFN_ROW_TILE, PROJ_ROW_TILE, OUT_ROW_TILE
    nx, nz = batch * seq, batch * ctx_len
    n_groups = batch + 1
    z0 = nx // ctx_len
    assert nx % ctx_len == 0 and all(seq % t == 0 and nz % t == 0 for t in (tf_, tp, to_))
    ffn_rows = dict(tiles_per_group=seq // tf_, n_groups=n_groups)
    proj_rows = dict(tiles_per_group=seq // tp, n_groups=n_groups, n_tiles=(nx + nz) // tp)
    pos_tiles = seq // tp

    n_cond = 8
    cond = jnp.concatenate([c, c_ctx[None, :], jnp.zeros((n_cond - batch - 1, d), F32)], axis=0)
    mods_all = mod_params(cond, w_mod, b_mod).reshape(depth, n_cond, N_MOD, d)

    qk_scale = HEAD_DIM ** -0.5 * LOG2E
    mla_scale = (MLA_NOPE + MLA_ROPE) ** -0.5 * LOG2E

    stream = None
    for li in range(depth):
        last = li == depth - 1
        j = li // N_MIXERS
        kind = li % N_MIXERS
        mods = mods_all[li]

        if li == 0:
            stream = jnp.concatenate([x.reshape(nx, d), ctx.reshape(nz, d)], axis=0)
        stream = ffn_half_step_f32_weights(stream, norm_g[li, 0], mods, ffn_w_in, ffn_w_out, (li, 0),
                                           k0=0, total_tiles=(nx + nz) // tf_, **ffn_rows)

        proj = dict(mods=mods, mod_idx=(3, 4), **proj_rows)
        if kind == 0:
            col_scale = jnp.concatenate([jnp.full((NA_HEADS * HEAD_DIM,), qk_scale, F32),
                                         jnp.ones((2 * NA_HEADS * HEAD_DIM,), F32)])
            qkv = norm_linear(stream, 0, d, norm_g[li, 1], na_w_qkv[j].astype(BF16),
                              col_scale=col_scale, tn=1024, **proj)
            t2 = na_bias_table(na_rpb[j])
            ox = na_attention(qkv, t2, batch=batch, seq=seq, ctx_len=ctx_len)
            if not last:
                hd = lambda part: (qkv, part * NA_HEADS, NA_HEADS)
                oz = ctx_attention([hd(0)], [hd(1)], hd(2), batch=batch, ctx_len=ctx_len,
                                   z0=z0, n_heads=NA_HEADS)
            w_o = na_w_o[j]
        elif kind == 1:
            w_down = mla_w_down[j]
            lora = MLA_Q_LORA + MLA_KV_LORA
            cqkv = norm_linear(stream, 0, d, norm_g[li, 1], w_down[:, :lora].astype(BF16),
                               layout="rows", out_dtype=F32, tn=lora, **proj)
            c_t, s1_t, s2_t = mla_rope_tables(seq, mla_scale, tp)
            w_kpe = jnp.pad(w_down[:, lora:], ((0, 0), (0, LANE - MLA_ROPE))).astype(BF16)
            kp = norm_linear(stream, 0, d, norm_g[li, 1], w_kpe, rope=(96, 32),
                             tables=(c_t[2:], s1_t[2:], s2_t[2:]), pos_tiles=pos_tiles,
                             tn=LANE, **proj)
            wq = mla_w_q_up[j].reshape(MLA_Q_LORA, MLA_HEADS, MLA_NOPE + MLA_ROPE)
            wq_n = wq[:, :, :MLA_NOPE].reshape(MLA_Q_LORA, MLA_HEADS * MLA_NOPE).astype(BF16)
            wq_p = jnp.pad(wq[:, :, MLA_NOPE:], ((0, 0), (0, 0), (0, LANE - MLA_ROPE)))
            wq_p = wq_p.reshape(MLA_Q_LORA, MLA_HEADS * LANE).astype(BF16)
            qn = norm_linear(cqkv, 0, MLA_Q_LORA, mla_q_norm_g[j], wq_n,
                             col_scale=jnp.full((MLA_HEADS * MLA_NOPE,), mla_scale, F32), tn=1024, **proj_rows)
            qp = norm_linear(cqkv, 0, MLA_Q_LORA, mla_q_norm_g[j], wq_p, rope=(96, 32),
                             tables=(c_t[1:2], s1_t[1:2], s2_t[1:2]), pos_tiles=pos_tiles, tn=512, **proj_rows)
            wkv = mla_w_kv_up[j].reshape(MLA_KV_LORA, MLA_HEADS, MLA_NOPE + MLA_V)
            w_kn = wkv[:, :, :MLA_NOPE].reshape(MLA_KV_LORA, -1).astype(BF16)
            w_v = wkv[:, :, MLA_NOPE:].reshape(MLA_KV_LORA, -1).astype(BF16)
            kn = norm_linear(cqkv, 1, MLA_KV_LORA, mla_kv_norm_g[j], w_kn, tn=1024, **proj_rows)
            vt = norm_linear(cqkv, 1, MLA_KV_LORA, mla_kv_norm_g[j], w_v, layout="heads_t", tn=1024, **proj_rows)
            ox = mla_attention(qn, qp, kn, kp, vt, batch=batch, seq=seq, ctx_len=ctx_len)
            if not last:
                oz = ctx_attention([(qn, 0, MLA_HEADS), (qp, 0, MLA_HEADS)],
                                   [(kn, 0, MLA_HEADS), (kp, 0, 1)], (vt, 0, MLA_HEADS),
                                   batch=batch, ctx_len=ctx_len, z0=z0, n_heads=MLA_HEADS, v_transposed=True)
            w_o = mla_w_o[j]
        else:
            c_t, s_t = swa_rope_tables(seq, qk_scale, tp)
            n_q_blocks = SWA_HEADS * HEAD_DIM // 512
            qkv = norm_linear(stream, 0, d, norm_g[li, 1], swa_w_qkv[j].astype(BF16), rope=(64,),
                              tables=(c_t, s_t), set_thresholds=(n_q_blocks, n_q_blocks + 1),
                              pos_tiles=pos_tiles, tn=512, **proj)
            ox = swa_attention(qkv, swa_sink[j], batch=batch, seq=seq, ctx_len=ctx_len)
            if not last:
                oz = ctx_attention([(qkv, 0, SWA_HEADS)], [(qkv, SWA_HEADS, SWA_KV_HEADS)],
                                   (qkv, SWA_HEADS + SWA_KV_HEADS, SWA_KV_HEADS),
                                   batch=batch, ctx_len=ctx_len, z0=z0, n_heads=SWA_HEADS,
                                   sink=swa_sink[j])
            w_o = swa_w_o[j]

        n_rows = nx if last else nx + nz
        stream = linear_residual(ox, w_o.astype(BF16), stream, mods, k_gate=5, n_tiles=n_rows // to_,
                                 a_tail=None if last else oz, tiles_per_group=seq // to_, n_groups=n_groups)
        stream = ffn_half_step_f32_weights(stream, norm_g[li, 2], mods, ffn_w_in, ffn_w_out, (li, 1), k0=6,
                                           total_tiles=n_rows // tf_,
                                           final_gain=final_norm_g if last else None, **ffn_rows)

    return stream.reshape(batch, seq, d)
```
